```python
import jax, jax.numpy as jnp
from jax import lax
import numpy as np

D_MODEL = 2048
BATCH = 1
SEQ = 16384
DEPTH = 1

GRID_W = 64
CTX_LEN = 256
N_HEADS_ATTN = 16
HEAD_DIM = 64
D_ATTN = N_HEADS_ATTN * HEAD_DIM
D_MIX = D_MODEL
D_CONV = D_MIX - D_ATTN
N_CONV_GROUPS = 16
CONV_WIDTH = 3
WIN_H = 8
WIN_W = 16
ROW_BLOCK = 2
D_FF = 5632
ROPE_BASE = 10000.0
EPS = 1e-6
N_MOD = 9
SPLITS = (D_ATTN, 2 * D_ATTN, 3 * D_ATTN, 3 * D_ATTN + D_CONV, 3 * D_ATTN + 2 * D_CONV)
D_IN = 3 * D_ATTN + 3 * D_CONV

kernel_name = "hybrid_na_shortconv_macaron_dit_layer"


def rmsnorm(x, g):
    xf = x.astype(jnp.float32)
    y = xf * lax.rsqrt(jnp.mean(xf * xf, axis=-1, keepdims=True) + EPS)
    return (y * g.astype(jnp.float32)).astype(x.dtype)


def modulate(h, shift, scale):
    return h * (1 + scale) + shift


def ada_mod(cvec, w_ada, b_ada):
    m = jax.nn.silu(cvec) @ w_ada + b_ada
    return jnp.split(m[..., None, :], N_MOD, axis=-1)


def half_ffn(x, mods, g_norm, w_in, w_out):
    shift, scale, gate = mods
    h = modulate(rmsnorm(x, g_norm), shift, scale)
    a, b = jnp.split(h @ w_in, 2, axis=-1)
    return x + 0.5 * gate * ((jax.nn.silu(a) * b) @ w_out)


def heads(t):
    return t.reshape(t.shape[0], t.shape[1], -1, HEAD_DIM)


def _rotate(xa, pos):
    nf = xa.shape[-1] // 2
    inv = ROPE_BASE ** (-jnp.arange(nf, dtype=jnp.float32) / nf)
    ang = pos.astype(jnp.float32)[:, None] * inv[None, :]
    cos = jnp.cos(ang)[None, :, None, :].astype(xa.dtype)
    sin = jnp.sin(ang)[None, :, None, :].astype(xa.dtype)
    x1, x2 = xa[..., :nf], xa[..., nf:]
    return jnp.concatenate([x1 * cos - x2 * sin, x1 * sin + x2 * cos], axis=-1)


def axial_rope(t, pos_r, pos_c):
    half = t.shape[-1] // 2
    return jnp.concatenate([_rotate(t[..., :half], pos_r), _rotate(t[..., half:], pos_c)], axis=-1)


def neighbourhood_attention(q, k, v, k_ctx, v_ctx, rpb, rows):
    B, S, H, Dh = q.shape
    kh = min(WIN_H, rows)
    n_blocks = rows // ROW_BLOCK
    scale = Dh ** -0.5
    qg = q.reshape(B, rows, GRID_W, H, Dh)
    kg = k.reshape(B, rows, GRID_W, H, Dh)
    vg = v.reshape(B, rows, GRID_W, H, Dh)
    cols = np.arange(GRID_W)
    col_start = np.clip(cols - WIN_W // 2, 0, GRID_W - WIN_W)
    col_idx = col_start[:, None] + np.arange(WIN_W)[None, :]
    dc = col_idx - cols[:, None] + (WIN_W - 1)

    def block(bi):
        r = bi * ROW_BLOCK + jnp.arange(ROW_BLOCK)
        rs = jnp.clip(r - kh // 2, 0, rows - kh)
        row_idx = rs[:, None] + jnp.arange(kh)[None, :]
        dr = row_idx - r[:, None] + (WIN_H - 1)
        qb = lax.dynamic_slice_in_dim(qg, bi * ROW_BLOCK, ROW_BLOCK, axis=1)
        kb = kg[:, row_idx][:, :, :, col_idx]
        vb = vg[:, row_idx][:, :, :, col_idx]
        bias = rpb[:, dr[:, None, :, None], dc[None, :, None, :]]
        bias = jnp.transpose(bias, (1, 2, 0, 3, 4))
        s_loc = jnp.einsum('brwhd,brkwjhd->brwhkj', qb, kb) * scale + bias
        s_ctx = jnp.einsum('brwhd,blhd->brwhl', qb, k_ctx) * scale
        s = jnp.concatenate([s_loc.reshape(B, ROW_BLOCK, GRID_W, H, kh * WIN_W), s_ctx], axis=-1)
        p = jax.nn.softmax(s.astype(jnp.float32), axis=-1).astype(v.dtype)
        p_loc = p[..., :kh * WIN_W].reshape(B, ROW_BLOCK, GRID_W, H, kh, WIN_W)
        p_ctx = p[..., kh * WIN_W:]
        return (jnp.einsum('brwhkj,brkwjhd->brwhd', p_loc, vb)
                + jnp.einsum('brwhl,blhd->brwhd', p_ctx, v_ctx))

    out = lax.map(block, jnp.arange(n_blocks))
    return jnp.moveaxis(out, 0, 1).reshape(B, S, H * Dh)


def context_attention(q, k, v):
    B, L, H, Dh = q.shape
    s = jnp.einsum('blhd,bmhd->bhlm', q, k) * (Dh ** -0.5)
    p = jax.nn.softmax(s.astype(jnp.float32), axis=-1).astype(v.dtype)
    return jnp.einsum('bhlm,bmhd->blhd', p, v).reshape(B, L, H * Dh)


def gated_short_conv(bg, cg, u, conv_w, conv_b):
    z = cg * u
    S = z.shape[1]
    pad = CONV_WIDTH // 2
    zp = jnp.pad(z, ((0, 0), (pad, pad), (0, 0)))
    y = conv_b
    for j in range(CONV_WIDTH):
        y = y + zp[:, j:j + S] * conv_w[j]
    return bg * y


def setup_inputs(seed: int = 0) -> dict:
    key = jax.random.key(seed)
    ks = jax.random.split(key, 24)
    f32 = jnp.float32
    L, D = DEPTH, D_MODEL

    def nrm(k, shape, s):
        return jax.random.normal(k, shape, f32) * s

    def gain(k, shape):
        return 1.0 + 0.05 * jax.random.normal(k, shape, f32)

    return {
        "x": nrm(ks[0], (BATCH, SEQ, D), 1.0),
        "c": nrm(ks[1], (BATCH, D), 1.0),
        "ctx": nrm(ks[2], (BATCH, CTX_LEN, D), 1.0),
        "c_ctx": nrm(ks[3], (D,), 1.0),
        "w_ada": nrm(ks[4], (L, D, N_MOD * D), 0.5 * D ** -0.5),
        "b_ada": nrm(ks[5], (L, N_MOD * D), 0.02),
        "ff1_norm": gain(ks[6], (L, D)),
        "ff1_w_in": nrm(ks[7], (L, D, 2 * D_FF), D ** -0.5),
        "ff1_w_out": nrm(ks[8], (L, D_FF, D), D_FF ** -0.5),
        "mix_norm": gain(ks[9], (L, D)),
        "w_in": nrm(ks[10], (L, D, D_IN), D ** -0.5),
        "q_norm": gain(ks[11], (L, HEAD_DIM)),
        "k_norm": gain(ks[12], (L, HEAD_DIM)),
        "rpb": nrm(ks[13], (L, N_HEADS_ATTN, 2 * WIN_H - 1, 2 * WIN_W - 1), 0.1),
        "conv_w": nrm(ks[14], (L, CONV_WIDTH, D_CONV), CONV_WIDTH ** -0.5),
        "conv_b": nrm(ks[15], (L, D_CONV), 0.02),
        "out_norm_attn": gain(ks[16], (L, D_ATTN)),
        "out_norm_conv": gain(ks[17], (L, D_CONV)),
        "w_out": nrm(ks[18], (L, D_MIX, D), D_MIX ** -0.5),
        "ff2_norm": gain(ks[19], (L, D)),
        "ff2_w_in": nrm(ks[20], (L, D, 2 * D_FF), D ** -0.5),
        "ff2_w_out": nrm(ks[21], (L, D_FF, D), D_FF ** -0.5),
    }


def reference(x, c, ctx, c_ctx, w_ada, b_ada, ff1_norm, ff1_w_in, ff1_w_out, mix_norm, w_in,
              q_norm, k_norm, rpb, conv_w, conv_b, out_norm_attn, out_norm_conv, w_out,
              ff2_norm, ff2_w_in, ff2_w_out):
    B, S, _ = x.shape
    rows = S // GRID_W
    pos = jnp.arange(S, dtype=jnp.int32)
    pos_r, pos_c = pos // GRID_W, pos % GRID_W

    for l in range(DEPTH):
        update_ctx = l < DEPTH - 1
        mx = ada_mod(c, w_ada[l], b_ada[l])
        mc = ada_mod(c_ctx[None], w_ada[l], b_ada[l])

        x = half_ffn(x, mx[0:3], ff1_norm[l], ff1_w_in[l], ff1_w_out[l])
        ctx = half_ffn(ctx, mc[0:3], ff1_norm[l], ff1_w_in[l], ff1_w_out[l])

        hx = modulate(rmsnorm(x, mix_norm[l]), mx[3], mx[4])
        hc = modulate(rmsnorm(ctx, mix_norm[l]), mc[3], mc[4])
        q, k, v, bg, cg, u = jnp.split(hx @ w_in[l], SPLITS, axis=-1)
        q = axial_rope(rmsnorm(heads(q), q_norm[l]), pos_r, pos_c)
        k = axial_rope(rmsnorm(heads(k), k_norm[l]), pos_r, pos_c)
        v = heads(v)
        if update_ctx:
            qc, kc, vc, bgc, cgc, uc = jnp.split(hc @ w_in[l], SPLITS, axis=-1)
        else:
            kc, vc = jnp.split(hc @ w_in[l][:, D_ATTN:3 * D_ATTN], 2, axis=-1)
        kc = rmsnorm(heads(kc), k_norm[l])
        vc = heads(vc)

        attn = neighbourhood_attention(q, k, v, kc, vc, rpb[l], rows)
        conv = gated_short_conv(bg, cg, u, conv_w[l], conv_b[l])
        y = jnp.concatenate([rmsnorm(attn, out_norm_attn[l]), rmsnorm(conv, out_norm_conv[l])],
                            axis=-1) @ w_out[l]
        if update_ctx:
            attn_c = context_attention(rmsnorm(heads(qc), q_norm[l]), kc, vc)
            conv_c = gated_short_conv(bgc, cgc, uc, conv_w[l], conv_b[l])
            yc = jnp.concatenate([rmsnorm(attn_c, out_norm_attn[l]), rmsnorm(conv_c, out_norm_conv[l])],
                                 axis=-1) @ w_out[l]
            ctx = ctx + mc[5] * yc
        x = x + mx[5] * y

        x = half_ffn(x, mx[6:9], ff2_norm[l], ff2_w_in[l], ff2_w_out[l])
        if update_ctx:
            ctx = half_ffn(ctx, mc[6:9], ff2_norm[l], ff2_w_in[l], ff2_w_out[l])
    return x
```

```python
import functools

import numpy as np
import jax
import jax.numpy as jnp
from jax import lax
from jax.experimental import pallas as pl
from jax.experimental.pallas import tpu as pltpu

GRID_W = 64
HEAD_DIM = 64
WIN_H = 8
WIN_W = 16
ROPE_BASE = 10000.0
EPS = 1e-6
N_MOD = 9

LANES = 128
HEADS_PER_SLAB = LANES // HEAD_DIM
VMEM_LIMIT = 56 * 1024 * 1024
NEG = -1e30

Q_ROWS = 8
K_ROWS = 16
NO_TILE = 2 * WIN_H - 1

f32 = jnp.float32
bf16 = jnp.bfloat16


def _params(*sem):
    return pltpu.CompilerParams(dimension_semantics=sem, vmem_limit_bytes=VMEM_LIMIT)


def _ada_kernel(c_ref, w_ref, b_ref, o_ref):
    c = c_ref[...]
    s = c * jax.nn.sigmoid(c)
    w = w_ref[...]
    r0 = jnp.sum(w * s[:, 0:1], axis=0, keepdims=True)
    r1 = jnp.sum(w * s[:, 1:2], axis=0, keepdims=True)
    o_ref[...] = jnp.concatenate([r0, r1], axis=0) + b_ref[...]


def _ada_call(cvecs_t, w_ada, b_ada, tn=1024):
    d, n = w_ada.shape
    return pl.pallas_call(
        _ada_kernel,
        grid=(n // tn,),
        in_specs=[
            pl.BlockSpec((d, 2), lambda j: (0, 0)),
            pl.BlockSpec((d, tn), lambda j: (0, j)),
            pl.BlockSpec((1, tn), lambda j: (0, j)),
        ],
        out_specs=pl.BlockSpec((2, tn), lambda j: (0, j)),
        out_shape=jax.ShapeDtypeStruct((2, n), f32),
        compiler_params=_params("arbitrary"),
        name="ada",
    )(cvecs_t, w_ada, b_ada.reshape(1, n))


def _norm_mod_into(x_ref, g_ref, mod_ref, h_ref, rows_per_chunk=256):
    tm = x_ref.shape[0]
    g = g_ref[...]
    shift = mod_ref[0:1, :]
    scale1p = 1.0 + mod_ref[1:2, :]
    rc = min(rows_per_chunk, tm)

    def body(r, carry):
        rows = pl.ds(pl.multiple_of(r * rc, rc), rc)
        x = x_ref[rows, :]
        ms = jnp.mean(x * x, axis=-1, keepdims=True)
        y = x * lax.rsqrt(ms + EPS) * g
        h_ref[rows, :] = (y * scale1p + shift).astype(bf16)
        return carry

    lax.fori_loop(0, tm // rc, body, 0)


def _ffn_kernel(x_ref, mod_ref, g_ref, wa_ref, wb_ref, wo_ref, o_ref, h_ref):
    j = pl.program_id(1)
    last = pl.num_programs(1) - 1

    @pl.when(j == 0)
    def _():
        _norm_mod_into(x_ref, g_ref, mod_ref, h_ref)

    h = h_ref[...]
    a = jnp.dot(h, wa_ref[...], preferred_element_type=f32)
    b = jnp.dot(h, wb_ref[...], preferred_element_type=f32)
    act = (a * jax.nn.sigmoid(a) * b).astype(bf16)
    d = jnp.dot(act, wo_ref[...], preferred_element_type=f32)

    @pl.when(j == 0)
    def _():
        o_ref[...] = d

    @pl.when(j > 0)
    def _():
        o_ref[...] += d

    @pl.when(j == last)
    def _():
        o_ref[...] = x_ref[...] + (0.5 * mod_ref[2:3, :]) * o_ref[...]


def _ffn_call(x, mods, g, w_in, w_out, tm, tf):
    m, d = x.shape
    ff = w_out.shape[0]
    nf = ff // tf
    return pl.pallas_call(
        _ffn_kernel,
        grid=(m // tm, nf),
        in_specs=[
            pl.BlockSpec((tm, d), lambda i, j: (i, 0)),
            pl.BlockSpec((3, d), lambda i, j: (0, 0)),
            pl.BlockSpec((1, d), lambda i, j: (0, 0)),
            pl.BlockSpec((d, tf), lambda i, j: (0, j)),
            pl.BlockSpec((d, tf), lambda i, j: (0, j + nf)),
            pl.BlockSpec((tf, d), lambda i, j: (j, 0)),
        ],
        out_specs=pl.BlockSpec((tm, d), lambda i, j: (i, 0)),
        out_shape=jax.ShapeDtypeStruct((m, d), f32),
        scratch_shapes=[pltpu.VMEM((tm, d), bf16)],
        compiler_params=_params("arbitrary", "arbitrary"),
        name="ffn",
    )(x, mods, g.reshape(1, d), w_in, w_in, w_out)


def _head_norm(y, gain, bd):
    ssq = jnp.dot((y * y).astype(bf16), bd, preferred_element_type=f32)
    return y * lax.rsqrt(ssq * (1.0 / HEAD_DIM) + EPS) * gain


def _proj_kernel(kinds, tn, *refs):
    use_rope = any(k in ("q", "k") for k in kinds)
    if use_rope:
        x_ref, mod_ref, g_ref, w_ref, qg_ref, kg_ref, cos_ref, sin_ref, o_ref, h_ref = refs
    else:
        x_ref, mod_ref, g_ref, w_ref, qg_ref, kg_ref, o_ref, h_ref = refs
    n = pl.program_id(1)

    @pl.when(n == 0)
    def _():
        _norm_mod_into(x_ref, g_ref, mod_ref, h_ref)

    res = jnp.dot(h_ref[...], w_ref[...], preferred_element_type=f32)

    row = lax.broadcasted_iota(jnp.int32, (LANES, LANES), 0) // HEAD_DIM
    col = lax.broadcasted_iota(jnp.int32, (LANES, LANES), 1) // HEAD_DIM
    bd = (row == col).astype(bf16)
    lane = lax.broadcasted_iota(jnp.int32, (1, LANES), 1)
    upper = (lane & (HEAD_DIM // 4)) != 0
    quarter = HEAD_DIM // 4

    def headwise(gain_ref, rope):
        gain = gain_ref[...]
        for s in range(tn // LANES):
            y = _head_norm(res[:, s * LANES:(s + 1) * LANES], gain, bd)
            if rope:
                partner = jnp.where(upper, pltpu.roll(y, quarter, 1), pltpu.roll(y, LANES - quarter, 1))
                y = y * cos_ref[...] + partner * sin_ref[...]
            o_ref[:, s * LANES:(s + 1) * LANES] = y.astype(o_ref.dtype)

    for idx, kind in enumerate(kinds):
        @pl.when(n == idx)
        def _(kind=kind):
            if kind == "q":
                headwise(qg_ref, True)
            elif kind == "k":
                headwise(kg_ref, True)
            elif kind == "k_nopos":
                headwise(kg_ref, False)
            elif kind == "plain":
                o_ref[...] = res.astype(o_ref.dtype)
            elif kind == "mul_lo":
                o_ref[:, :tn // 2] = (res[:, :tn // 2] * res[:, tn // 2:]).astype(o_ref.dtype)
            elif kind == "mul_hi":
                o_ref[:, tn // 2:] = (res[:, :tn // 2] * res[:, tn // 2:]).astype(o_ref.dtype)
            else:
                raise ValueError(kind)


def _proj_call(x, mods, g, w, qg, kg, cos, sin, kinds, out_cols, tm, tn=1024):
    m, d = x.shape
    use_rope = cos is not None
    n_out = max(out_cols) + 1
    out_cols = tuple(out_cols)

    def out_map(i, n):
        blk = jnp.int32(out_cols[-1])
        for idx in range(len(out_cols) - 2, -1, -1):
            blk = jnp.where(n == idx, out_cols[idx], blk)
        return (i, blk)

    in_specs = [
        pl.BlockSpec((tm, d), lambda i, n: (i, 0)),
        pl.BlockSpec((2, d), lambda i, n: (0, 0)),
        pl.BlockSpec((1, d), lambda i, n: (0, 0)),
        pl.BlockSpec((d, tn), lambda i, n: (0, n)),
        pl.BlockSpec((1, LANES), lambda i, n: (0, 0)),
        pl.BlockSpec((1, LANES), lambda i, n: (0, 0)),
    ]
    args = [x, mods, g.reshape(1, d), w, qg, kg]
    if use_rope:
        in_specs += [pl.BlockSpec((tm, LANES), lambda i, n: (i, 0))] * 2
        args += [cos, sin]
    return pl.pallas_call(
        functools.partial(_proj_kernel, tuple(kinds), tn),
        grid=(m // tm, len(kinds)),
        in_specs=in_specs,
        out_specs=pl.BlockSpec((tm, tn), out_map),
        out_shape=jax.ShapeDtypeStruct((m, n_out * tn), bf16),
        scratch_shapes=[pltpu.VMEM((tm, d), bf16)],
        compiler_params=_params("arbitrary", "arbitrary"),
        name="proj",
    )(*args)


def _row_window_tables(rows):
    nblk = rows // Q_ROWS
    tables = []
    for i in (0, 1, nblk - 1):
        r0 = Q_ROWS * i
        ks = min(max(r0 - WIN_H // 2, 0), rows - K_ROWS)
        tab = [[NO_TILE] * K_ROWS for _ in range(Q_ROWS)]
        for a in range(Q_ROWS):
            r = r0 + a
            rs = min(max(r - WIN_H // 2, 0), rows - WIN_H)
            assert ks <= rs and rs + WIN_H <= ks + K_ROWS
            for b in range(K_ROWS):
                kr = ks + b
                if rs <= kr < rs + WIN_H:
                    tab[a][b] = kr - r + (WIN_H - 1)
        tables.append(tab)
    return tables


def _attn_kernel(rows, q_ref, k_ref, v_ref, kc_ref, vc_ref, t_ref, o_ref, bias_ref):
    i = pl.program_id(1)
    nblk = pl.num_programs(1)
    tq = Q_ROWS * GRID_W
    tk = K_ROWS * GRID_W
    lane = lax.broadcasted_iota(jnp.int32, (1, LANES), 1)
    first_head = lane < HEAD_DIM
    tables = _row_window_tables(rows)

    def assemble(tab):
        for hh in range(HEADS_PER_SLAB):
            for a in range(Q_ROWS):
                for bp in range(K_ROWS // 2):
                    e0, e1 = tab[a][2 * bp], tab[a][2 * bp + 1]
                    if e0 == NO_TILE and e1 == NO_TILE:
                        tile = jnp.full((GRID_W, LANES), NEG, f32)
                    else:
                        tile = jnp.where(first_head, t_ref[hh, e0], t_ref[hh, e1])
                    bias_ref[hh, a * GRID_W:(a + 1) * GRID_W, bp * LANES:(bp + 1) * LANES] = tile

    @pl.when(i == 0)
    def _():
        assemble(tables[0])

    @pl.when(i == 1)
    def _():
        assemble(tables[1])

    @pl.when(i == nblk - 1)
    def _():
        assemble(tables[2])

    ks = jnp.clip(i * Q_ROWS - WIN_H // 2, 0, rows - K_ROWS) * GRID_W
    ks = pl.multiple_of(ks, (WIN_H // 2) * GRID_W)
    kw = k_ref[pl.ds(ks, tk), :]
    vw = v_ref[pl.ds(ks, tk), :]
    kc = kc_ref[...]
    vc = vc_ref[...]
    q2 = q_ref[...] * jnp.asarray(HEAD_DIM ** -0.5, bf16)

    nt = (((1,), (1,)), ((), ()))
    outs = []
    for hh in range(HEADS_PER_SLAB):
        mine = first_head if hh == 0 else jnp.logical_not(first_head)
        qh = jnp.where(mine, q2, jnp.zeros_like(q2))
        s_loc = lax.dot_general(qh, kw, nt, preferred_element_type=f32) + bias_ref[hh]
        s_ctx = lax.dot_general(qh, kc, nt, preferred_element_type=f32)
        mx = jnp.maximum(jnp.max(s_loc, axis=-1, keepdims=True), jnp.max(s_ctx, axis=-1, keepdims=True))
        p_loc = jnp.exp(s_loc - mx)
        p_ctx = jnp.exp(s_ctx - mx)
        den = jnp.sum(p_loc, axis=-1, keepdims=True) + jnp.sum(p_ctx, axis=-1, keepdims=True)
        o = (jnp.dot(p_loc.astype(bf16), vw, preferred_element_type=f32)
             + jnp.dot(p_ctx.astype(bf16), vc, preferred_element_type=f32))
        outs.append(o / den)
    o_ref[...] = jnp.where(first_head, outs[0], outs[1]).astype(o_ref.dtype)


def _attn_call(qkv, kvc, txd, n_heads, rows):
    s = qkv.shape[0]
    d_attn = n_heads * HEAD_DIM
    slabs = d_attn // LANES
    tq = Q_ROWS * GRID_W
    nblk = rows // Q_ROWS
    assert rows % Q_ROWS == 0 and nblk >= 3 and rows >= K_ROWS
    lc = kvc.shape[0]
    return pl.pallas_call(
        functools.partial(_attn_kernel, rows),
        grid=(slabs, nblk),
        in_specs=[
            pl.BlockSpec((tq, LANES), lambda h, i: (i, h)),
            pl.BlockSpec((s, LANES), lambda h, i: (0, slabs + h)),
            pl.BlockSpec((s, LANES), lambda h, i: (0, 2 * slabs + h)),
            pl.BlockSpec((lc, LANES), lambda h, i: (0, h)),
            pl.BlockSpec((lc, LANES), lambda h, i: (0, slabs + h)),
            pl.BlockSpec((HEADS_PER_SLAB, 2 * WIN_H, GRID_W, LANES), lambda h, i: (h, 0, 0, 0)),
        ],
        out_specs=pl.BlockSpec((tq, LANES), lambda h, i: (i, h)),
        out_shape=jax.ShapeDtypeStruct((s, d_attn), bf16),
        scratch_shapes=[pltpu.VMEM((HEADS_PER_SLAB, tq, K_ROWS * GRID_W), f32)],
        compiler_params=_params("arbitrary", "arbitrary"),
        name="attn",
    )(qkv, qkv, qkv, kvc, kvc, txd)


def _rms(x, g):
    ms = jnp.mean(x * x, axis=-1, keepdims=True)
    return x * lax.rsqrt(ms + EPS) * g


def _out_kernel(attn_ref, bg_ref, z_ref, zp_ref, zn_ref, x_ref, cw_ref, cb_ref, ga_ref, gc_ref, gate_ref,
                w_ref, o_ref):
    i = pl.program_id(0)
    nt = pl.num_programs(0)
    tm, dc = z_ref.shape
    da = attn_ref.shape[1]
    z = z_ref[...].astype(f32)
    halo = zp_ref.shape[0]
    before = jnp.where(i > 0, zp_ref[halo - 1:halo, :].astype(f32), 0.0)
    after = jnp.where(i < nt - 1, zn_ref[0:1, :].astype(f32), 0.0)
    row = lax.broadcasted_iota(jnp.int32, (tm, 1), 0)
    z_m = jnp.where(row == 0, before, pltpu.roll(z, 1, 0))
    z_p = jnp.where(row == tm - 1, after, pltpu.roll(z, tm - 1, 0))
    y = cb_ref[...] + z_m * cw_ref[0:1, :]
    y = y + z * cw_ref[1:2, :]
    y = y + z_p * cw_ref[2:3, :]
    conv = bg_ref[...].astype(f32) * y
    cn = _rms(conv, gc_ref[...]).astype(bf16)
    an = _rms(attn_ref[...].astype(f32), ga_ref[...]).astype(bf16)
    proj = (jnp.dot(an, w_ref[0:da, :], preferred_element_type=f32)
            + jnp.dot(cn, w_ref[da:da + dc, :], preferred_element_type=f32))
    o_ref[...] = x_ref[...] + gate_ref[...] * proj


def _out_call(attn, mixed, x, conv_w, conv_b, ga, gc, gate, w_out, tm, bg_col, z_col, halo=16):
    s, d = x.shape
    da = attn.shape[1]
    dc = conv_w.shape[1]
    per = tm // halo
    nh = s // halo
    return pl.pallas_call(
        _out_kernel,
        grid=(s // tm,),
        in_specs=[
            pl.BlockSpec((tm, da), lambda i: (i, 0)),
            pl.BlockSpec((tm, dc), lambda i: (i, bg_col)),
            pl.BlockSpec((tm, dc), lambda i: (i, z_col)),
            pl.BlockSpec((halo, dc), lambda i: (jnp.maximum(i * per - 1, 0), z_col)),
            pl.BlockSpec((halo, dc), lambda i: (jnp.minimum((i + 1) * per, nh - 1), z_col)),
            pl.BlockSpec((tm, d), lambda i: (i, 0)),
            pl.BlockSpec((3, dc), lambda i: (0, 0)),
            pl.BlockSpec((1, dc), lambda i: (0, 0)),
            pl.BlockSpec((1, da), lambda i: (0, 0)),
            pl.BlockSpec((1, dc), lambda i: (0, 0)),
            pl.BlockSpec((1, d), lambda i: (0, 0)),
            pl.BlockSpec((da + dc, d), lambda i: (0, 0)),
        ],
        out_specs=pl.BlockSpec((tm, d), lambda i: (i, 0)),
        out_shape=jax.ShapeDtypeStruct((s, d), f32),
        compiler_params=_params("arbitrary"),
        name="out",
    )(attn, mixed, mixed, mixed, mixed, x, conv_w, conv_b.reshape(1, dc), ga.reshape(1, da),
      gc.reshape(1, dc), gate.reshape(1, d), w_out)


def _rope_tables(s):
    nf = HEAD_DIM // 4
    inv = ROPE_BASE ** (-jnp.arange(nf, dtype=f32) / nf)
    inv_l = jnp.tile(inv, LANES // nf)[None, :]
    pos = jnp.arange(s, dtype=jnp.int32)
    pos_r = (pos // GRID_W).astype(f32)[:, None]
    pos_c = (pos % GRID_W).astype(f32)[:, None]
    lane = np.arange(LANES)
    by_row = jnp.asarray((lane % HEAD_DIM) < HEAD_DIM // 2)[None, :]
    ang = jnp.where(by_row, pos_r, pos_c) * inv_l
    sign = jnp.asarray(np.where((lane & nf) != 0, 1.0, -1.0).astype(np.float32))[None, :]
    return jnp.cos(ang), jnp.sin(ang) * sign


def _bias_tiles(rpb):
    qc = np.arange(GRID_W)[:, None]
    kc = np.arange(GRID_W)[None, :]
    cs = np.clip(qc - WIN_W // 2, 0, GRID_W - WIN_W)
    valid = (kc >= cs) & (kc < cs + WIN_W)
    dc = np.clip(kc - qc + WIN_W - 1, 0, 2 * WIN_W - 2)
    t = jnp.where(jnp.asarray(valid), rpb[:, :, dc], NEG)
    t = jnp.concatenate([t, jnp.full_like(t[:, :1], NEG)], axis=1)
    return jnp.concatenate([t, t], axis=-1)


def kernel(x, c, ctx, c_ctx, w_ada, b_ada, ff1_norm, ff1_w_in, ff1_w_out, mix_norm, w_in, q_norm, k_norm,
           rpb, conv_w, conv_b, out_norm_attn, out_norm_conv, w_out, ff2_norm, ff2_w_in, ff2_w_out):
    batch, s, d = x.shape
    depth = w_ada.shape[0]
    assert batch == 1 and depth == 1
    n_heads = rpb.shape[1]
    d_attn = n_heads * HEAD_DIM
    d_conv = conv_w.shape[-1]
    rows = s // GRID_W
    half = d_conv // 2

    xs = x[0]
    cs = ctx[0]
    l = 0

    mods = _ada_call(jnp.stack([c[0], c_ctx], axis=1), w_ada[l], b_ada[l])
    mx = mods[0].reshape(N_MOD, d)
    mc = mods[1].reshape(N_MOD, d)

    w1_in = ff1_w_in[l].astype(bf16)
    w1_out = ff1_w_out[l].astype(bf16)
    w2_in = ff2_w_in[l].astype(bf16)
    w2_out = ff2_w_out[l].astype(bf16)
    wl = w_in[l]
    o_bg, o_cg, o_u = 3 * d_attn, 3 * d_attn + d_conv, 3 * d_attn + 2 * d_conv
    w_mix = jnp.concatenate([
        wl[:, :o_cg],
        wl[:, o_cg:o_cg + half], wl[:, o_u:o_u + half],
        wl[:, o_cg + half:o_u], wl[:, o_u + half:],
    ], axis=1).astype(bf16)
    w_kv = wl[:, d_attn:3 * d_attn].astype(bf16)
    w_o = w_out[l].astype(bf16)

    qg = jnp.tile(q_norm[l], HEADS_PER_SLAB)[None, :]
    kg = jnp.tile(k_norm[l], HEADS_PER_SLAB)[None, :]
    cos, sin = _rope_tables(s)
    txd = _bias_tiles(rpb[l])

    xs = _ffn_call(xs, mx[0:3], ff1_norm[l], w1_in, w1_out, tm=512, tf=512)
    cs = _ffn_call(cs, mc[0:3], ff1_norm[l], w1_in, w1_out, tm=cs.shape[0], tf=512)

    mixed = _proj_call(xs, mx[3:5], mix_norm[l], w_mix, qg, kg, cos, sin,
                       kinds=("q", "k", "plain", "plain", "mul_lo", "mul_hi"),
                       out_cols=(0, 1, 2, 3, 4, 4), tm=1024)
    kvc = _proj_call(cs, mc[3:5], mix_norm[l], w_kv, qg, kg, None, None,
                     kinds=("k_nopos", "plain"), out_cols=(0, 1), tm=cs.shape[0])

    attn = _attn_call(mixed, kvc, txd, n_heads, rows)
    xs = _out_call(attn, mixed, xs, conv_w[l], conv_b[l], out_norm_attn[l], out_norm_conv[l], mx[5], w_o,
                   tm=512, bg_col=3, z_col=4)

    xs = _ffn_call(xs, mx[6:9], ff2_norm[l], w2_in, w2_out, tm=512, tf=512)
    return xs[None]
```

```python
import functools

import numpy as np
import jax
import jax.numpy as jnp
from jax import lax
from jax.experimental import pallas as pl
from jax.experimental.pallas import tpu as pltpu

GRID_W = 64
HEAD_DIM = 64
WIN_H = 8
WIN_W = 16
ROPE_BASE = 10000.0
EPS = 1e-6
N_MOD = 9

LANES = 128
HEADS_PER_SLAB = LANES // HEAD_DIM
VMEM_LIMIT = 60 * 1024 * 1024
NEG = -1e30

Q_ROWS = 8
K_ROWS = 16
N_TILES = 2 * WIN_H
NO_TILE = N_TILES - 1
SOFTMAX_ROWS = 16
LOGIT_BOUND = 20.0

f32 = jnp.float32
bf16 = jnp.bfloat16


def _params(*sem, flags=None):
    return pltpu.CompilerParams(dimension_semantics=sem, vmem_limit_bytes=VMEM_LIMIT, flags=flags)


def _pick(n, values):
    out = jnp.int32(values[-1])
    for idx in range(len(values) - 2, -1, -1):
        out = jnp.where(n == idx, values[idx], out)
    return out


def _ada_kernel(c_ref, w_ref, b_ref, o_ref):
    c = c_ref[...]
    s = c * jax.nn.sigmoid(c)
    w = w_ref[...]
    r0 = jnp.sum(w * s[:, 0:1], axis=0, keepdims=True)
    r1 = jnp.sum(w * s[:, 1:2], axis=0, keepdims=True)
    o_ref[...] = jnp.concatenate([r0, r1], axis=0) + b_ref[...]


def _ada_call(cvecs_t, w_ada, b_ada, tn=1024):
    d, n = w_ada.shape
    return pl.pallas_call(
        _ada_kernel,
        grid=(n // tn,),
        in_specs=[
            pl.BlockSpec((d, 2), lambda j: (0, 0)),
            pl.BlockSpec((d, tn), lambda j: (0, j)),
            pl.BlockSpec((1, tn), lambda j: (0, j)),
        ],
        out_specs=pl.BlockSpec((2, tn), lambda j: (0, j)),
        out_shape=jax.ShapeDtypeStruct((2, n), f32),
        compiler_params=_params("arbitrary"),
        name="ada",
    )(cvecs_t, w_ada, b_ada.reshape(1, n))


def _norm_mod_into(x_ref, g_ref, mod_ref, h_ref, copy_ref=None, rows_per_chunk=256):
    tm = x_ref.shape[0]
    g = g_ref[...]
    shift = mod_ref[0:1, :]
    scale1p = 1.0 + mod_ref[1:2, :]
    rc = min(rows_per_chunk, tm)

    def body(r, carry):
        rows = pl.ds(pl.multiple_of(r * rc, rc), rc)
        x = x_ref[rows, :]
        ms = jnp.mean(x * x, axis=-1, keepdims=True)
        y = x * lax.rsqrt(ms + EPS) * g
        h_ref[rows, :] = (y * scale1p + shift).astype(bf16)
        if copy_ref is not None:
            copy_ref[rows, :] = x
        return carry

    lax.fori_loop(0, tm // rc, body, 0)


def _ffn_kernel(n_split, x_ref, mod_ref, g_ref, wa_ref, wb_ref, wo_ref, o_ref, h_ref):
    @pl.when(pl.program_id(1) == 0)
    def _():
        _norm_mod_into(x_ref, g_ref, mod_ref, h_ref, copy_ref=o_ref)

    h = h_ref[...]
    a = jnp.dot(h, wa_ref[...], preferred_element_type=f32)
    b = jnp.dot(h, wb_ref[...], preferred_element_type=f32)
    act = (a * jax.nn.sigmoid(a) * b).astype(bf16)
    half_gate = 0.5 * mod_ref[2:3, :]
    w = o_ref.shape[1] // n_split
    for c in range(n_split):
        cols = slice(c * w, (c + 1) * w)
        o_ref[:, cols] += half_gate[:, cols] * jnp.dot(act, wo_ref[:, cols], preferred_element_type=f32)


def _ffn_call(x, mods, g, w_in, w_out, tm, tf, n_split=4):
    m, d = x.shape
    ff = w_out.shape[0]
    nf = ff // tf
    return pl.pallas_call(
        functools.partial(_ffn_kernel, n_split),
        grid=(m // tm, nf),
        in_specs=[
            pl.BlockSpec((tm, d), lambda i, j: (i, 0)),
            pl.BlockSpec((3, d), lambda i, j: (0, 0)),
            pl.BlockSpec((1, d), lambda i, j: (0, 0)),
            pl.BlockSpec((d, tf), lambda i, j: (0, j)),
            pl.BlockSpec((d, tf), lambda i, j: (0, j + nf)),
            pl.BlockSpec((tf, d), lambda i, j: (j, 0)),
        ],
        out_specs=pl.BlockSpec((tm, d), lambda i, j: (i, 0)),
        out_shape=jax.ShapeDtypeStruct((m, d), f32),
        scratch_shapes=[pltpu.VMEM((tm, d), bf16)],
        compiler_params=_params("arbitrary", "arbitrary"),
        name="ffn",
    )(x, mods, g.reshape(1, d), w_in, w_in, w_out)


def _proj_kernel(kinds, *refs):
    use_rope = any(k in ("q", "k") for k in kinds)
    if use_rope:
        x_ref, mod_ref, g_ref, wa_ref, wb_ref, qg_ref, kg_ref, cos_ref, sin_ref, o_ref, h_ref = refs
    else:
        x_ref, mod_ref, g_ref, wa_ref, wb_ref, qg_ref, kg_ref, o_ref, h_ref = refs
    n = pl.program_id(1)
    tw = wa_ref.shape[1]

    @pl.when(n == 0)
    def _():
        _norm_mod_into(x_ref, g_ref, mod_ref, h_ref)

    h = h_ref[...]
    res = (jnp.dot(h, wa_ref[...], preferred_element_type=f32),
           jnp.dot(h, wb_ref[...], preferred_element_type=f32))

    pair = 2 * LANES
    row = lax.broadcasted_iota(jnp.int32, (pair, pair), 0) // HEAD_DIM
    col = lax.broadcasted_iota(jnp.int32, (pair, pair), 1) // HEAD_DIM
    same_head = (row == col).astype(bf16)
    lane = lax.broadcasted_iota(jnp.int32, (1, LANES), 1)
    quarter = HEAD_DIM // 4
    upper = (lane & quarter) != 0

    def headwise(gain_ref, rope):
        gain = gain_ref[...]
        for part in range(2):
            for s in range(tw // pair):
                y = res[part][:, s * pair:(s + 1) * pair]
                ssq = jnp.dot((y * y).astype(bf16), same_head, preferred_element_type=f32)
                y = y * lax.rsqrt(ssq * (1.0 / HEAD_DIM) + EPS) * gain
                for u in range(2):
                    yu = y[:, u * LANES:(u + 1) * LANES]
                    if rope:
                        partner = jnp.where(upper, pltpu.roll(yu, quarter, 1), pltpu.roll(yu, LANES - quarter, 1))
                        yu = yu * cos_ref[...] + partner * sin_ref[...]
                    lo = part * tw + s * pair + u * LANES
                    o_ref[:, lo:lo + LANES] = yu.astype(o_ref.dtype)

    for idx, kind in enumerate(kinds):
        @pl.when(n == idx)
        def _(kind=kind):
            if kind == "q":
                headwise(qg_ref, True)
            elif kind == "k":
                headwise(kg_ref, True)
            elif kind == "k_nopos":
                headwise(kg_ref, False)
            elif kind == "plain":
                o_ref[:, :tw] = res[0].astype(o_ref.dtype)
                o_ref[:, tw:] = res[1].astype(o_ref.dtype)
            elif kind == "mul_lo":
                o_ref[:, :tw] = (res[0] * res[1]).astype(o_ref.dtype)
            elif kind == "mul_hi":
                o_ref[:, tw:] = (res[0] * res[1]).astype(o_ref.dtype)
            else:
                raise ValueError(kind)


def _proj_call(x, mods, g, w, qg, kg, cos, sin, kinds, w_blocks, out_cols, tm, tn=1024):
    m, d = x.shape
    tw = tn // 2
    use_rope = cos is not None
    n_out = max(out_cols) + 1
    blk_a = tuple(b[0] for b in w_blocks)
    blk_b = tuple(b[1] for b in w_blocks)
    out_cols = tuple(out_cols)
    pair_gain = lambda v: jnp.tile(v, 2 * HEADS_PER_SLAB)[None, :]

    in_specs = [
        pl.BlockSpec((tm, d), lambda i, n: (i, 0)),
        pl.BlockSpec((2, d), lambda i, n: (0, 0)),
        pl.BlockSpec((1, d), lambda i, n: (0, 0)),
        pl.BlockSpec((d, tw), lambda i, n: (0, _pick(n, blk_a))),
        pl.BlockSpec((d, tw), lambda i, n: (0, _pick(n, blk_b))),
        pl.BlockSpec((1, 2 * LANES), lambda i, n: (0, 0)),
        pl.BlockSpec((1, 2 * LANES), lambda i, n: (0, 0)),
    ]
    args = [x, mods, g.reshape(1, d), w, w, pair_gain(qg), pair_gain(kg)]
    if use_rope:
        in_specs += [pl.BlockSpec((tm, LANES), lambda i, n: (i, 0))] * 2
        args += [cos, sin]
    return pl.pallas_call(
        functools.partial(_proj_kernel, tuple(kinds)),
        grid=(m // tm, len(kinds)),
        in_specs=in_specs,
        out_specs=pl.BlockSpec((tm, tn), lambda i, n: (i, _pick(n, out_cols))),
        out_shape=jax.ShapeDtypeStruct((m, n_out * tn), bf16),
        scratch_shapes=[pltpu.VMEM((tm, d), bf16)],
        compiler_params=_params("arbitrary", "arbitrary"),
        name="proj",
    )(*args)


def _row_window_tables(rows):
    nblk = rows // Q_ROWS
    tables = []
    for i in (0, 1, nblk - 1):
        r0 = Q_ROWS * i
        ks = min(max(r0 - WIN_H // 2, 0), rows - K_ROWS)
        tab = [[NO_TILE] * K_ROWS for _ in range(Q_ROWS)]
        for a in range(Q_ROWS):
            r = r0 + a
            rs = min(max(r - WIN_H // 2, 0), rows - WIN_H)
            assert ks <= rs and rs + WIN_H <= ks + K_ROWS
            for b in range(K_ROWS):
                kr = ks + b
                if rs <= kr < rs + WIN_H:
                    tab[a][b] = kr - r + (WIN_H - 1)
        tables.append(tab)
    return tables


def _attn_kernel(rows, bounded_ref, q_ref, k_ref, v_ref, kc_ref, vc_ref, rpb_ref, o_ref,
                 tile_ref, bias_ref, kall_ref, vall_ref, s_ref, p_ref, m_ref):
    i = pl.program_id(1)
    nblk = pl.num_programs(1)
    tq = Q_ROWS * GRID_W
    tk = K_ROWS * GRID_W
    lane = lax.broadcasted_iota(jnp.int32, (1, LANES), 1)
    first_head = lane < HEAD_DIM
    tables = _row_window_tables(rows)

    def build_tiles():
        qc = lax.broadcasted_iota(jnp.int32, (GRID_W, LANES), 0)
        kc = lax.broadcasted_iota(jnp.int32, (GRID_W, LANES), 1) % GRID_W
        cs = jnp.clip(qc - WIN_W // 2, 0, GRID_W - WIN_W)
        in_window = (kc >= cs) & (kc < cs + WIN_W)
        for hh in range(HEADS_PER_SLAB):
            for e in range(NO_TILE):
                base = jnp.broadcast_to(rpb_ref[hh, e:e + 1, :], (GRID_W, LANES))
                shifted = pltpu.roll(base, LANES - (WIN_W - 1), 1, stride=1, stride_axis=0)
                tile_ref[hh, e] = jnp.where(in_window, shifted, NEG)
            tile_ref[hh, NO_TILE] = jnp.full((GRID_W, LANES), NEG, f32)

    def assemble(tab):
        for hh in range(HEADS_PER_SLAB):
            for a in range(Q_ROWS):
                for bp in range(K_ROWS // 2):
                    e0, e1 = tab[a][2 * bp], tab[a][2 * bp + 1]
                    if e0 == e1:
                        tile = tile_ref[hh, e0]
                    else:
                        tile = jnp.where(first_head, tile_ref[hh, e0], tile_ref[hh, e1])
                    bias_ref[hh, a * GRID_W:(a + 1) * GRID_W, bp * LANES:(bp + 1) * LANES] = tile

    @pl.when(i == 0)
    def _():
        build_tiles()
        assemble(tables[0])
        kall_ref[tk:, :] = kc_ref[...]
        vall_ref[tk:, :] = vc_ref[...]

    @pl.when(i == 1)
    def _():
        assemble(tables[1])

    @pl.when(i == nblk - 1)
    def _():
        assemble(tables[2])

    ks = jnp.clip(i * Q_ROWS - WIN_H // 2, 0, rows - K_ROWS) * GRID_W
    ks = pl.multiple_of(ks, (WIN_H // 2) * GRID_W)
    kall_ref[0:tk, :] = k_ref[pl.ds(ks, tk), :]
    vall_ref[0:tk, :] = v_ref[pl.ds(ks, tk), :]
    q2 = q_ref[...] * jnp.asarray(HEAD_DIM ** -0.5, bf16)

    nt = (((1,), (1,)), ((), ()))
    key_tile = 2 * LANES
    n_loc = tk // LANES
    n_all = kall_ref.shape[0] // LANES
    groups = [slice(r * SOFTMAX_ROWS, (r + 1) * SOFTMAX_ROWS) for r in range(tq // SOFTMAX_ROWS)]

    def head_query(hh):
        mine = first_head if hh == 0 else jnp.logical_not(first_head)
        return mine, jnp.where(mine, q2, jnp.zeros_like(q2))

    def write_out(numer_and_sums):
        done = [o * (1.0 / pltpu.roll(o, HEAD_DIM, 1)) for o in numer_and_sums]
        o_ref[...] = jnp.where(first_head, done[0], done[1]).astype(o_ref.dtype)

    @pl.when(bounded_ref[0] != 0)
    def _():
        outs = []
        for hh in range(HEADS_PER_SLAB):
            mine, qh = head_query(hh)
            acc = None
            for kt in range(kall_ref.shape[0] // key_tile):
                keys = slice(kt * key_tile, (kt + 1) * key_tile)
                t = lax.dot_general(qh, kall_ref[keys, :], nt, preferred_element_type=f32)
                if kt * key_tile < tk:
                    t = t + bias_ref[hh, :, keys]
                vh = jnp.where(mine, vall_ref[keys, :], jnp.ones((key_tile, LANES), bf16))
                d = jnp.dot(jnp.exp(t).astype(bf16), vh, preferred_element_type=f32)
                acc = d if acc is None else acc + d
            outs.append(acc)
        write_out(outs)

    @pl.when(bounded_ref[0] == 0)
    def _():
        def logits(hh, rws, c, minus=None):
            cols = slice(c * LANES, (c + 1) * LANES)
            t = s_ref[hh, rws, cols]
            if minus is not None:
                t = t - minus
            return t + bias_ref[hh, rws, cols] if c < n_loc else t

        outs = []
        for hh in range(HEADS_PER_SLAB):
            mine, qh = head_query(hh)
            s_ref[hh] = lax.dot_general(qh, kall_ref[...], nt, preferred_element_type=f32)
            for rws in groups:
                acc = logits(hh, rws, 0)
                for c in range(1, n_all):
                    acc = jnp.maximum(acc, logits(hh, rws, c))
                m_ref[hh, rws, :] = jnp.broadcast_to(jnp.max(acc, axis=-1, keepdims=True), (SOFTMAX_ROWS, LANES))
            for rws in groups:
                m = m_ref[hh, rws, :]
                for c in range(n_all):
                    p_ref[hh, rws, c * LANES:(c + 1) * LANES] = jnp.exp(logits(hh, rws, c, minus=m)).astype(bf16)
            vh = jnp.where(mine, vall_ref[...], jnp.ones(vall_ref.shape, bf16))
            outs.append(jnp.dot(p_ref[hh], vh, preferred_element_type=f32))
        write_out(outs)


def _attn_call(bounded, qkv, kvc, rpb_lanes, n_heads, rows):
    s = qkv.shape[0]
    d_attn = n_heads * HEAD_DIM
    slabs = d_attn // LANES
    tq = Q_ROWS * GRID_W
    tk = K_ROWS * GRID_W
    nblk = rows // Q_ROWS
    assert rows % Q_ROWS == 0 and nblk >= 3 and rows >= K_ROWS
    lc = kvc.shape[0]
    assert (tk + lc) % (2 * LANES) == 0
    grid_spec = pltpu.PrefetchScalarGridSpec(
        num_scalar_prefetch=1,
        grid=(slabs, nblk),
        in_specs=[
            pl.BlockSpec((tq, LANES), lambda h, i, b: (i, h)),
            pl.BlockSpec((s, LANES), lambda h, i, b: (0, slabs + h)),
            pl.BlockSpec((s, LANES), lambda h, i, b: (0, 2 * slabs + h)),
            pl.BlockSpec((lc, LANES), lambda h, i, b: (0, h)),
            pl.BlockSpec((lc, LANES), lambda h, i, b: (0, slabs + h)),
            pl.BlockSpec((HEADS_PER_SLAB, N_TILES, LANES), lambda h, i, b: (h, 0, 0)),
        ],
        out_specs=pl.BlockSpec((tq, LANES), lambda h, i, b: (i, h)),
        scratch_shapes=[
            pltpu.VMEM((HEADS_PER_SLAB, N_TILES, GRID_W, LANES), f32),
            pltpu.VMEM((HEADS_PER_SLAB, tq, tk), f32),
            pltpu.VMEM((tk + lc, LANES), bf16),
            pltpu.VMEM((tk + lc, LANES), bf16),
            pltpu.VMEM((HEADS_PER_SLAB, tq, tk + lc), f32),
            pltpu.VMEM((HEADS_PER_SLAB, tq, tk + lc), bf16),
            pltpu.VMEM((HEADS_PER_SLAB, tq, LANES), f32),
        ],
    )
    return pl.pallas_call(
        functools.partial(_attn_kernel, rows),
        grid_spec=grid_spec,
        out_shape=jax.ShapeDtypeStruct((s, d_attn), bf16),
        compiler_params=_params("arbitrary", "arbitrary"),
        name="attn",
    )(bounded, qkv, qkv, qkv, kvc, kvc, rpb_lanes)


def _rms(x, g):
    ms = jnp.mean(x * x, axis=-1, keepdims=True)
    return x * lax.rsqrt(ms + EPS) * g


def _out_kernel(attn_ref, bg_ref, z_ref, zp_ref, zn_ref, x_ref, cw_ref, cb_ref, ga_ref, gc_ref, gate_ref,
                w_ref, o_ref):
    i = pl.program_id(0)
    nt = pl.num_programs(0)
    tm, dc = z_ref.shape
    da = attn_ref.shape[1]
    z = z_ref[...].astype(f32)
    halo = zp_ref.shape[0]
    before = jnp.where(i > 0, zp_ref[halo - 1:halo, :].astype(f32), 0.0)
    after = jnp.where(i < nt - 1, zn_ref[0:1, :].astype(f32), 0.0)
    row = lax.broadcasted_iota(jnp.int32, (tm, 1), 0)
    z_m = jnp.where(row == 0, before, pltpu.roll(z, 1, 0))
    z_p = jnp.where(row == tm - 1, after, pltpu.roll(z, tm - 1, 0))
    y = cb_ref[...] + z_m * cw_ref[0:1, :]
    y = y + z * cw_ref[1:2, :]
    y = y + z_p * cw_ref[2:3, :]
    conv = bg_ref[...].astype(f32) * y
    cn = _rms(conv, gc_ref[...]).astype(bf16)
    an = _rms(attn_ref[...].astype(f32), ga_ref[...]).astype(bf16)
    proj = (jnp.dot(an, w_ref[0:da, :], preferred_element_type=f32)
            + jnp.dot(cn, w_ref[da:da + dc, :], preferred_element_type=f32))
    o_ref[...] = x_ref[...] + gate_ref[...] * proj


def _out_call(attn, mixed, x, conv_w, conv_b, ga, gc, gate, w_out, tm, bg_col, z_col, halo=16):
    s, d = x.shape
    da = attn.shape[1]
    dc = conv_w.shape[1]
    per = tm // halo
    nh = s // halo
    return pl.pallas_call(
        _out_kernel,
        grid=(s // tm,),
        in_specs=[
            pl.BlockSpec((tm, da), lambda i: (i, 0)),
            pl.BlockSpec((tm, dc), lambda i: (i, bg_col)),
            pl.BlockSpec((tm, dc), lambda i: (i, z_col)),
            pl.BlockSpec((halo, dc), lambda i: (jnp.maximum(i * per - 1, 0), z_col)),
            pl.BlockSpec((halo, dc), lambda i: (jnp.minimum((i + 1) * per, nh - 1), z_col)),
            pl.BlockSpec((tm, d), lambda i: (i, 0)),
            pl.BlockSpec((3, dc), lambda i: (0, 0)),
            pl.BlockSpec((1, dc), lambda i: (0, 0)),
            pl.BlockSpec((1, da), lambda i: (0, 0)),
            pl.BlockSpec((1, dc), lambda i: (0, 0)),
            pl.BlockSpec((1, d), lambda i: (0, 0)),
            pl.BlockSpec((da + dc, d), lambda i: (0, 0)),
        ],
        out_specs=pl.BlockSpec((tm, d), lambda i: (i, 0)),
        out_shape=jax.ShapeDtypeStruct((s, d), f32),
        compiler_params=_params("arbitrary"),
        name="out",
    )(attn, mixed, mixed, mixed, mixed, x, conv_w, conv_b.reshape(1, dc), ga.reshape(1, da),
      gc.reshape(1, dc), gate.reshape(1, d), w_out)


def _rope_tables(rows):
    nf = HEAD_DIM // 4
    inv = ROPE_BASE ** (-jnp.arange(nf, dtype=f32) / nf)
    ang_r = jnp.arange(rows, dtype=jnp.int32).astype(f32)[:, None] * inv[None, :]
    ang_c = jnp.arange(GRID_W, dtype=jnp.int32).astype(f32)[:, None] * inv[None, :]

    def expand(by_row, by_col, lower_sign):
        r = jnp.broadcast_to(by_row[:, None, :], (rows, GRID_W, nf))
        c = jnp.broadcast_to(by_col[None, :, :], (rows, GRID_W, nf))
        head = jnp.concatenate([lower_sign * r, r, lower_sign * c, c], axis=-1)
        return jnp.tile(head, (1, 1, HEADS_PER_SLAB)).reshape(rows * GRID_W, LANES)

    return expand(jnp.cos(ang_r), jnp.cos(ang_c), 1.0), expand(jnp.sin(ang_r), jnp.sin(ang_c), -1.0)


def _logits_bounded(q_gain, k_gain, rpb):
    bound = (HEAD_DIM ** 0.5) * 1.02 * jnp.max(jnp.abs(q_gain)) * jnp.max(jnp.abs(k_gain)) + jnp.max(jnp.abs(rpb))
    return (bound <= LOGIT_BOUND).astype(jnp.int32).reshape(1)


def _rpb_lanes(rpb):
    h, nr, ncol = rpb.shape
    half = jnp.pad(rpb, ((0, 0), (0, N_TILES - nr), (0, GRID_W - ncol)))
    return jnp.concatenate([half, half], axis=-1)


def kernel(x, c, ctx, c_ctx, w_ada, b_ada, ff1_norm, ff1_w_in, ff1_w_out, mix_norm, w_in, q_norm, k_norm,
           rpb, conv_w, conv_b, out_norm_attn, out_norm_conv, w_out, ff2_norm, ff2_w_in, ff2_w_out):
    batch, s, d = x.shape
    depth = w_ada.shape[0]
    assert batch == 1 and depth == 1
    n_heads = rpb.shape[1]
    d_attn = n_heads * HEAD_DIM
    d_conv = conv_w.shape[-1]
    assert d_attn == 1024 and d_conv == 1024
    rows = s // GRID_W

    xs = x[0]
    cs = ctx[0]
    l = 0

    mods = _ada_call(jnp.stack([c[0], c_ctx], axis=1), w_ada[l], b_ada[l])
    mx = mods[0].reshape(N_MOD, d)
    mc = mods[1].reshape(N_MOD, d)

    w1_in = ff1_w_in[l].astype(bf16)
    w1_out = ff1_w_out[l].astype(bf16)
    w2_in = ff2_w_in[l].astype(bf16)
    w2_out = ff2_w_out[l].astype(bf16)
    w_mix = w_in[l].astype(bf16)
    w_o = w_out[l].astype(bf16)

    cos, sin = _rope_tables(rows)

    xs = _ffn_call(xs, mx[0:3], ff1_norm[l], w1_in, w1_out, tm=1024, tf=512)
    cs = _ffn_call(cs, mc[0:3], ff1_norm[l], w1_in, w1_out, tm=cs.shape[0], tf=512)

    mixed = _proj_call(xs, mx[3:5], mix_norm[l], w_mix, q_norm[l], k_norm[l], cos, sin,
                       kinds=("q", "k", "plain", "plain", "mul_lo", "mul_hi"),
                       w_blocks=((0, 1), (2, 3), (4, 5), (6, 7), (8, 10), (9, 11)),
                       out_cols=(0, 1, 2, 3, 4, 4), tm=1024)
    kvc = _proj_call(cs, mc[3:5], mix_norm[l], w_mix, q_norm[l], k_norm[l], None, None,
                     kinds=("k_nopos", "plain"), w_blocks=((2, 3), (4, 5)), out_cols=(0, 1), tm=cs.shape[0])

    attn = _attn_call(_logits_bounded(q_norm[l], k_norm[l], rpb[l]), mixed, kvc, _rpb_lanes(rpb[l]), n_heads, rows)
    xs = _out_call(attn, mixed, xs, conv_w[l], conv_b[l], out_norm_attn[l], out_norm_conv[l], mx[5], w_o,
                   tm=512, bg_col=3, z_col=4)

    xs = _ffn_call(xs, mx[6:9], ff2_norm[l], w2_in, w2_out, tm=1024, tf=512)
    return xs[None]
```

```python
import functools

import numpy as np
import jax
import jax.numpy as jnp
from jax import lax
from jax.experimental import pallas as pl
from jax.experimental.pallas import tpu as pltpu

GRID_W = 64
HEAD_DIM = 64
WIN_H = 8
WIN_W = 16
ROPE_BASE = 10000.0
EPS = 1e-6
N_MOD = 9

LANES = 128
HEADS_PER_SLAB = LANES // HEAD_DIM
VMEM_LIMIT = 60 * 1024 * 1024
NEG = -1e30

Q_ROWS = 8
K_ROWS = 16
N_TILES = 2 * WIN_H
NO_TILE = N_TILES - 1
SOFTMAX_ROWS = 16
LOGIT_BOUND = 20.0

f32 = jnp.float32
bf16 = jnp.bfloat16


def _params(*sem, flags=None):
    return pltpu.CompilerParams(dimension_semantics=sem, vmem_limit_bytes=VMEM_LIMIT, flags=flags)


def _pick(n, values):
    out = jnp.int32(values[-1])
    for idx in range(len(values) - 2, -1, -1):
        out = jnp.where(n == idx, values[idx], out)
    return out


def _ada_kernel(c_ref, w_ref, b_ref, o_ref):
    c = c_ref[...]
    s = c * jax.nn.sigmoid(c)
    w = w_ref[...]
    r0 = jnp.sum(w * s[:, 0:1], axis=0, keepdims=True)
    r1 = jnp.sum(w * s[:, 1:2], axis=0, keepdims=True)
    o_ref[...] = jnp.concatenate([r0, r1], axis=0) + b_ref[...]


def _ada_call(cvecs_t, w_ada, b_ada, tn=1024):
    d, n = w_ada.shape
    return pl.pallas_call(
        _ada_kernel,
        grid=(n // tn,),
        in_specs=[
            pl.BlockSpec((d, 2), lambda j: (0, 0)),
            pl.BlockSpec((d, tn), lambda j: (0, j)),
            pl.BlockSpec((1, tn), lambda j: (0, j)),
        ],
        out_specs=pl.BlockSpec((2, tn), lambda j: (0, j)),
        out_shape=jax.ShapeDtypeStruct((2, n), f32),
        compiler_params=_params("arbitrary"),
        name="ada",
    )(cvecs_t, w_ada, b_ada.reshape(1, n))


def _norm_mod_into(x_ref, g_ref, mod_ref, h_ref, inv_ref, copy_ref=None, rows_per_chunk=256):
    tm, d = x_ref.shape
    rc = min(rows_per_chunk, tm)

    def inv_rms(r, carry):
        rows = pl.ds(pl.multiple_of(r * rc, rc), rc)
        x = x_ref[rows, :]
        ms = jnp.sum(x * x, axis=-1, keepdims=True) * (1.0 / d)
        inv_ref[rows, :] = jnp.broadcast_to(lax.rsqrt(ms + EPS), (rc, LANES))
        return carry

    def scale(r, carry):
        rows = pl.ds(pl.multiple_of(r * rc, rc), rc)
        inv = inv_ref[rows, :]
        for c in range(d // LANES):
            cols = slice(c * LANES, (c + 1) * LANES)
            x = x_ref[rows, cols]
            y = x * inv * g_ref[:, cols]
            h_ref[rows, cols] = (y * (1.0 + mod_ref[1:2, cols]) + mod_ref[0:1, cols]).astype(bf16)
            if copy_ref is not None:
                copy_ref[rows, cols] = x
        return carry

    lax.fori_loop(0, tm // rc, inv_rms, 0)
    lax.fori_loop(0, tm // rc, scale, 0)


def _ffn_kernel(n_split, x_ref, mod_ref, g_ref, wa_ref, wb_ref, wo_ref, o_ref, h_ref, inv_ref):
    @pl.when(pl.program_id(1) == 0)
    def _():
        _norm_mod_into(x_ref, g_ref, mod_ref, h_ref, inv_ref, copy_ref=o_ref)

    h = h_ref[...]
    a = jnp.dot(h, wa_ref[...], preferred_element_type=f32)
    b = jnp.dot(h, wb_ref[...], preferred_element_type=f32)
    act = (a * jax.nn.sigmoid(a) * b).astype(bf16)
    half_gate = 0.5 * mod_ref[2:3, :]
    w = o_ref.shape[1] // n_split
    for c in range(n_split):
        cols = slice(c * w, (c + 1) * w)
        o_ref[:, cols] += half_gate[:, cols] * jnp.dot(act, wo_ref[:, cols], preferred_element_type=f32)


def _ffn_call(x, mods, g, w_in, w_out, tm, tf, n_split=4):
    m, d = x.shape
    ff = w_out.shape[0]
    nf = ff // tf
    return pl.pallas_call(
        functools.partial(_ffn_kernel, n_split),
        grid=(m // tm, nf),
        in_specs=[
            pl.BlockSpec((tm, d), lambda i, j: (i, 0)),
            pl.BlockSpec((3, d), lambda i, j: (0, 0)),
            pl.BlockSpec((1, d), lambda i, j: (0, 0)),
            pl.BlockSpec((d, tf), lambda i, j: (0, j)),
            pl.BlockSpec((d, tf), lambda i, j: (0, j + nf)),
            pl.BlockSpec((tf, d), lambda i, j: (j, 0)),
        ],
        out_specs=pl.BlockSpec((tm, d), lambda i, j: (i, 0)),
        out_shape=jax.ShapeDtypeStruct((m, d), f32),
        scratch_shapes=[pltpu.VMEM((tm, d), bf16), pltpu.VMEM((tm, LANES), f32)],
        compiler_params=_params("arbitrary", "arbitrary"),
        name="ffn",
    )(x, mods, g.reshape(1, d), w_in, w_in, w_out)


def _proj_kernel(kinds, *refs):
    use_rope = any(k in ("q", "k") for k in kinds)
    if use_rope:
        (x_ref, mod_ref, g_ref, wa_ref, wb_ref, qg_ref, kg_ref, rrow_ref, rcol_ref,
         o_ref, h_ref, inv_ref, cos_ref, sin_ref) = refs
    else:
        x_ref, mod_ref, g_ref, wa_ref, wb_ref, qg_ref, kg_ref, o_ref, h_ref, inv_ref = refs
    i = pl.program_id(0)
    n = pl.program_id(1)
    tm = o_ref.shape[0]
    tw = wa_ref.shape[1]

    @pl.when(n == 0)
    def _():
        _norm_mod_into(x_ref, g_ref, mod_ref, h_ref, inv_ref)
        if use_rope:
            for r in range(tm // GRID_W):
                tok = slice(r * GRID_W, (r + 1) * GRID_W)
                grid_row = i * (tm // GRID_W) + r
                cos_ref[tok, :] = rrow_ref[0, pl.ds(grid_row, 1), :] + rcol_ref[0]
                sin_ref[tok, :] = rrow_ref[1, pl.ds(grid_row, 1), :] + rcol_ref[1]

    h = h_ref[...]
    res = (jnp.dot(h, wa_ref[...], preferred_element_type=f32),
           jnp.dot(h, wb_ref[...], preferred_element_type=f32))

    pair = 2 * LANES
    row = lax.broadcasted_iota(jnp.int32, (pair, pair), 0) // HEAD_DIM
    col = lax.broadcasted_iota(jnp.int32, (pair, pair), 1) // HEAD_DIM
    same_head = (row == col).astype(bf16)
    lane = lax.broadcasted_iota(jnp.int32, (1, LANES), 1)
    quarter = HEAD_DIM // 4
    upper = (lane & quarter) != 0

    def headwise(gain_ref, rope):
        gain = gain_ref[...]
        for part in range(2):
            for s in range(tw // pair):
                y = res[part][:, s * pair:(s + 1) * pair]
                ssq = jnp.dot((y * y).astype(bf16), same_head, preferred_element_type=f32)
                y = y * lax.rsqrt(ssq * (1.0 / HEAD_DIM) + EPS) * gain
                for u in range(2):
                    yu = y[:, u * LANES:(u + 1) * LANES]
                    if rope:
                        partner = jnp.where(upper, pltpu.roll(yu, quarter, 1), pltpu.roll(yu, LANES - quarter, 1))
                        yu = yu * cos_ref[...] + partner * sin_ref[...]
                    lo = part * tw + s * pair + u * LANES
                    o_ref[:, lo:lo + LANES] = yu.astype(o_ref.dtype)

    for idx, kind in enumerate(kinds):
        @pl.when(n == idx)
        def _(kind=kind):
            if kind == "q":
                headwise(qg_ref, True)
            elif kind == "k":
                headwise(kg_ref, True)
            elif kind == "k_nopos":
                headwise(kg_ref, False)
            elif kind == "plain":
                o_ref[:, :tw] = res[0].astype(o_ref.dtype)
                o_ref[:, tw:] = res[1].astype(o_ref.dtype)
            elif kind == "mul_lo":
                o_ref[:, :tw] = (res[0] * res[1]).astype(o_ref.dtype)
            elif kind == "mul_hi":
                o_ref[:, tw:] = (res[0] * res[1]).astype(o_ref.dtype)
            else:
                raise ValueError(kind)


def _proj_call(x, mods, g, w, qg, kg, rope, kinds, w_blocks, out_cols, tm, tn=1024):
    m, d = x.shape
    tw = tn // 2
    use_rope = rope is not None
    n_out = max(out_cols) + 1
    blk_a = tuple(b[0] for b in w_blocks)
    blk_b = tuple(b[1] for b in w_blocks)
    out_cols = tuple(out_cols)
    pair_gain = lambda v: jnp.tile(v, 2 * HEADS_PER_SLAB)[None, :]
    assert tm % GRID_W == 0

    in_specs = [
        pl.BlockSpec((tm, d), lambda i, n: (i, 0)),
        pl.BlockSpec((2, d), lambda i, n: (0, 0)),
        pl.BlockSpec((1, d), lambda i, n: (0, 0)),
        pl.BlockSpec((d, tw), lambda i, n: (0, _pick(n, blk_a))),
        pl.BlockSpec((d, tw), lambda i, n: (0, _pick(n, blk_b))),
        pl.BlockSpec((1, 2 * LANES), lambda i, n: (0, 0)),
        pl.BlockSpec((1, 2 * LANES), lambda i, n: (0, 0)),
    ]
    args = [x, mods, g.reshape(1, d), w, w, pair_gain(qg), pair_gain(kg)]
    scratch = [pltpu.VMEM((tm, d), bf16), pltpu.VMEM((tm, LANES), f32)]
    if use_rope:
        in_specs += [pl.BlockSpec(t.shape, lambda i, n: (0, 0, 0)) for t in rope]
        args += list(rope)
        scratch += [pltpu.VMEM((tm, LANES), f32)] * 2
    return pl.pallas_call(
        functools.partial(_proj_kernel, tuple(kinds)),
        grid=(m // tm, len(kinds)),
        in_specs=in_specs,
        out_specs=pl.BlockSpec((tm, tn), lambda i, n: (i, _pick(n, out_cols))),
        out_shape=jax.ShapeDtypeStruct((m, n_out * tn), bf16),
        scratch_shapes=scratch,
        compiler_params=_params("arbitrary", "arbitrary"),
        name="proj",
    )(*args)


def _row_window_tables(rows):
    nblk = rows // Q_ROWS
    tables = []
    for i in (0, 1, nblk - 1):
        r0 = Q_ROWS * i
        ks = min(max(r0 - WIN_H // 2, 0), rows - K_ROWS)
        tab = [[NO_TILE] * K_ROWS for _ in range(Q_ROWS)]
        for a in range(Q_ROWS):
            r = r0 + a
            rs = min(max(r - WIN_H // 2, 0), rows - WIN_H)
            assert ks <= rs and rs + WIN_H <= ks + K_ROWS
            for b in range(K_ROWS):
                kr = ks + b
                if rs <= kr < rs + WIN_H:
                    tab[a][b] = kr - r + (WIN_H - 1)
        tables.append(tab)
    return tables


def _attn_kernel(rows, bounded_ref, q_ref, k_ref, v_ref, kc_ref, vc_ref, rpb_ref, o_ref,
                 tile_ref, bias_ref, kall_ref, vall_ref, s_ref, p_ref, m_ref):
    i = pl.program_id(1)
    nblk = pl.num_programs(1)
    tq = Q_ROWS * GRID_W
    tk = K_ROWS * GRID_W
    lane = lax.broadcasted_iota(jnp.int32, (1, LANES), 1)
    first_head = lane < HEAD_DIM
    tables = _row_window_tables(rows)

    def build_tiles():
        qc = lax.broadcasted_iota(jnp.int32, (GRID_W, LANES), 0)
        kc = lax.broadcasted_iota(jnp.int32, (GRID_W, LANES), 1) % GRID_W
        cs = jnp.clip(qc - WIN_W // 2, 0, GRID_W - WIN_W)
        in_window = (kc >= cs) & (kc < cs + WIN_W)
        for hh in range(HEADS_PER_SLAB):
            for e in range(NO_TILE):
                base = jnp.broadcast_to(rpb_ref[hh, e:e + 1, :], (GRID_W, LANES))
                shifted = pltpu.roll(base, LANES - (WIN_W - 1), 1, stride=1, stride_axis=0)
                tile_ref[hh, e] = jnp.where(in_window, shifted, NEG)
            tile_ref[hh, NO_TILE] = jnp.full((GRID_W, LANES), NEG, f32)

    def assemble(tab):
        for hh in range(HEADS_PER_SLAB):
            for a in range(Q_ROWS):
                for bp in range(K_ROWS // 2):
                    e0, e1 = tab[a][2 * bp], tab[a][2 * bp + 1]
                    if e0 == e1:
                        tile = tile_ref[hh, e0]
                    else:
                        tile = jnp.where(first_head, tile_ref[hh, e0], tile_ref[hh, e1])
                    bias_ref[hh, a * GRID_W:(a + 1) * GRID_W, bp * LANES:(bp + 1) * LANES] = tile

    @pl.when(i == 0)
    def _():
        build_tiles()
        assemble(tables[0])
        kall_ref[tk:, :] = kc_ref[...]
        vall_ref[tk:, :] = vc_ref[...]

    @pl.when(i == 1)
    def _():
        assemble(tables[1])

    @pl.when(i == nblk - 1)
    def _():
        assemble(tables[2])

    ks = jnp.clip(i * Q_ROWS - WIN_H // 2, 0, rows - K_ROWS) * GRID_W
    ks = pl.multiple_of(ks, (WIN_H // 2) * GRID_W)
    q2 = q_ref[...] * jnp.asarray(HEAD_DIM ** -0.5, bf16)

    nt = (((1,), (1,)), ((), ()))
    key_tile = 2 * LANES
    n_loc = tk // LANES
    n_all = kall_ref.shape[0] // LANES
    groups = [slice(r * SOFTMAX_ROWS, (r + 1) * SOFTMAX_ROWS) for r in range(tq // SOFTMAX_ROWS)]

    def head_query(hh):
        mine = first_head if hh == 0 else jnp.logical_not(first_head)
        return mine, jnp.where(mine, q2, jnp.zeros_like(q2))

    def write_out(numer_and_sums):
        done = [o * (1.0 / pltpu.roll(o, HEAD_DIM, 1)) for o in numer_and_sums]
        o_ref[...] = jnp.where(first_head, done[0], done[1]).astype(o_ref.dtype)

    @pl.when(bounded_ref[0] != 0)
    def _():
        outs = []
        for hh in range(HEADS_PER_SLAB):
            mine, qh = head_query(hh)
            acc = None
            for kt in range(kall_ref.shape[0] // key_tile):
                lo = kt * key_tile
                if lo < tk:
                    keys = pl.ds(ks + lo, key_tile)
                    k_tile, v_tile = k_ref[keys, :], v_ref[keys, :]
                else:
                    k_tile, v_tile = kc_ref[lo - tk:lo - tk + key_tile, :], vc_ref[lo - tk:lo - tk + key_tile, :]
                t = lax.dot_general(qh, k_tile, nt, preferred_element_type=f32)
                if lo < tk:
                    t = t + bias_ref[hh, :, lo:lo + key_tile]
                vh = jnp.where(mine, v_tile, jnp.ones((key_tile, LANES), bf16))
                d = jnp.dot(jnp.exp(t).astype(bf16), vh, preferred_element_type=f32)
                acc = d if acc is None else acc + d
            outs.append(acc)
        write_out(outs)

    @pl.when(bounded_ref[0] == 0)
    def _():
        def logits(hh, rws, c, minus=None):
            cols = slice(c * LANES, (c + 1) * LANES)
            t = s_ref[hh, rws, cols]
            if minus is not None:
                t = t - minus
            return t + bias_ref[hh, rws, cols] if c < n_loc else t

        kall_ref[0:tk, :] = k_ref[pl.ds(ks, tk), :]
        vall_ref[0:tk, :] = v_ref[pl.ds(ks, tk), :]
        outs = []
        for hh in range(HEADS_PER_SLAB):
            mine, qh = head_query(hh)
            s_ref[hh] = lax.dot_general(qh, kall_ref[...], nt, preferred_element_type=f32)
            for rws in groups:
                acc = logits(hh, rws, 0)
                for c in range(1, n_all):
                    acc = jnp.maximum(acc, logits(hh, rws, c))
                m_ref[hh, rws, :] = jnp.broadcast_to(jnp.max(acc, axis=-1, keepdims=True), (SOFTMAX_ROWS, LANES))
            for rws in groups:
                m = m_ref[hh, rws, :]
                for c in range(n_all):
                    p_ref[hh, rws, c * LANES:(c + 1) * LANES] = jnp.exp(logits(hh, rws, c, minus=m)).astype(bf16)
            vh = jnp.where(mine, vall_ref[...], jnp.ones(vall_ref.shape, bf16))
            outs.append(jnp.dot(p_ref[hh], vh, preferred_element_type=f32))
        write_out(outs)


def _attn_call(bounded, qkv, kvc, rpb_lanes, n_heads, rows):
    s = qkv.shape[0]
    d_attn = n_heads * HEAD_DIM
    slabs = d_attn // LANES
    tq = Q_ROWS * GRID_W
    tk = K_ROWS * GRID_W
    nblk = rows // Q_ROWS
    assert rows % Q_ROWS == 0 and nblk >= 3 and rows >= K_ROWS
    lc = kvc.shape[0]
    assert tk % (2 * LANES) == 0 and lc % (2 * LANES) == 0
    grid_spec = pltpu.PrefetchScalarGridSpec(
        num_scalar_prefetch=1,
        grid=(slabs, nblk),
        in_specs=[
            pl.BlockSpec((tq, LANES), lambda h, i, b: (i, h)),
            pl.BlockSpec((s, LANES), lambda h, i, b: (0, slabs + h)),
            pl.BlockSpec((s, LANES), lambda h, i, b: (0, 2 * slabs + h)),
            pl.BlockSpec((lc, LANES), lambda h, i, b: (0, h)),
            pl.BlockSpec((lc, LANES), lambda h, i, b: (0, slabs + h)),
            pl.BlockSpec((HEADS_PER_SLAB, N_TILES, LANES), lambda h, i, b: (h, 0, 0)),
        ],
        out_specs=pl.BlockSpec((tq, LANES), lambda h, i, b: (i, h)),
        scratch_shapes=[
            pltpu.VMEM((HEADS_PER_SLAB, N_TILES, GRID_W, LANES), f32),
            pltpu.VMEM((HEADS_PER_SLAB, tq, tk), f32),
            pltpu.VMEM((tk + lc, LANES), bf16),
            pltpu.VMEM((tk + lc, LANES), bf16),
            pltpu.VMEM((HEADS_PER_SLAB, tq, tk + lc), f32),
            pltpu.VMEM((HEADS_PER_SLAB, tq, tk + lc), bf16),
            pltpu.VMEM((HEADS_PER_SLAB, tq, LANES), f32),
        ],
    )
    return pl.pallas_call(
        functools.partial(_attn_kernel, rows),
        grid_spec=grid_spec,
        out_shape=jax.ShapeDtypeStruct((s, d_attn), bf16),
        compiler_params=_params("arbitrary", "arbitrary"),
        name="attn",
    )(bounded, qkv, qkv, qkv, kvc, kvc, rpb_lanes)


def _rms(x, g):
    ms = jnp.mean(x * x, axis=-1, keepdims=True)
    return x * lax.rsqrt(ms + EPS) * g


def _out_kernel(attn_ref, bg_ref, z_ref, zp_ref, zn_ref, x_ref, cw_ref, cb_ref, ga_ref, gc_ref, gate_ref,
                w_ref, o_ref):
    i = pl.program_id(0)
    nt = pl.num_programs(0)
    tm, dc = z_ref.shape
    da = attn_ref.shape[1]
    z = z_ref[...].astype(f32)
    halo = zp_ref.shape[0]
    before = jnp.where(i > 0, zp_ref[halo - 1:halo, :].astype(f32), 0.0)
    after = jnp.where(i < nt - 1, zn_ref[0:1, :].astype(f32), 0.0)
    row = lax.broadcasted_iota(jnp.int32, (tm, 1), 0)
    z_m = jnp.where(row == 0, before, pltpu.roll(z, 1, 0))
    z_p = jnp.where(row == tm - 1, after, pltpu.roll(z, tm - 1, 0))
    y = cb_ref[...] + z_m * cw_ref[0:1, :]
    y = y + z * cw_ref[1:2, :]
    y = y + z_p * cw_ref[2:3, :]
    conv = bg_ref[...].astype(f32) * y
    cn = _rms(conv, gc_ref[...]).astype(bf16)
    an = _rms(attn_ref[...].astype(f32), ga_ref[...]).astype(bf16)
    proj = (jnp.dot(an, w_ref[0:da, :], preferred_element_type=f32)
            + jnp.dot(cn, w_ref[da:da + dc, :], preferred_element_type=f32))
    o_ref[...] = x_ref[...] + gate_ref[...] * proj


def _out_call(attn, mixed, x, conv_w, conv_b, ga, gc, gate, w_out, tm, bg_col, z_col, halo=16):
    s, d = x.shape
    da = attn.shape[1]
    dc = conv_w.shape[1]
    per = tm // halo
    nh = s // halo
    return pl.pallas_call(
        _out_kernel,
        grid=(s // tm,),
        in_specs=[
            pl.BlockSpec((tm, da), lambda i: (i, 0)),
            pl.BlockSpec((tm, dc), lambda i: (i, bg_col)),
            pl.BlockSpec((tm, dc), lambda i: (i, z_col)),
            pl.BlockSpec((halo, dc), lambda i: (jnp.maximum(i * per - 1, 0), z_col)),
            pl.BlockSpec((halo, dc), lambda i: (jnp.minimum((i + 1) * per, nh - 1), z_col)),
            pl.BlockSpec((tm, d), lambda i: (i, 0)),
            pl.BlockSpec((3, dc), lambda i: (0, 0)),
            pl.BlockSpec((1, dc), lambda i: (0, 0)),
            pl.BlockSpec((1, da), lambda i: (0, 0)),
            pl.BlockSpec((1, dc), lambda i: (0, 0)),
            pl.BlockSpec((1, d), lambda i: (0, 0)),
            pl.BlockSpec((da + dc, d), lambda i: (0, 0)),
        ],
        out_specs=pl.BlockSpec((tm, d), lambda i: (i, 0)),
        out_shape=jax.ShapeDtypeStruct((s, d), f32),
        compiler_params=_params("arbitrary"),
        name="out",
    )(attn, mixed, mixed, mixed, mixed, x, conv_w, conv_b.reshape(1, dc), ga.reshape(1, da),
      gc.reshape(1, dc), gate.reshape(1, d), w_out)


def _rope_tables(rows):
    nf = HEAD_DIM // 4
    inv = ROPE_BASE ** (-jnp.arange(nf, dtype=f32) / nf)
    zero = jnp.zeros((1, nf), f32)

    def lanes(pos, first_half):
        ang = jnp.arange(pos, dtype=jnp.int32).astype(f32)[:, None] * inv[None, :]
        cos, sin = jnp.cos(ang), jnp.sin(ang)
        z = jnp.broadcast_to(zero, cos.shape)
        cos_head = [cos, cos, z, z] if first_half else [z, z, cos, cos]
        sin_head = [-sin, sin, z, z] if first_half else [z, z, -sin, sin]
        table = lambda parts: jnp.tile(jnp.concatenate(parts, axis=-1), (1, HEADS_PER_SLAB))
        return jnp.stack([table(cos_head), table(sin_head)])

    return lanes(rows, True), lanes(GRID_W, False)


def _logits_bounded(q_gain, k_gain, rpb):
    bound = (HEAD_DIM ** 0.5) * 1.02 * jnp.max(jnp.abs(q_gain)) * jnp.max(jnp.abs(k_gain)) + jnp.max(jnp.abs(rpb))
    return (bound <= LOGIT_BOUND).astype(jnp.int32).reshape(1)


def _rpb_lanes(rpb):
    h, nr, ncol = rpb.shape
    half = jnp.pad(rpb, ((0, 0), (0, N_TILES - nr), (0, GRID_W - ncol)))
    return jnp.concatenate([half, half], axis=-1)


def kernel(x, c, ctx, c_ctx, w_ada, b_ada, ff1_norm, ff1_w_in, ff1_w_out, mix_norm, w_in, q_norm, k_norm,
           rpb, conv_w, conv_b, out_norm_attn, out_norm_conv, w_out, ff2_norm, ff2_w_in, ff2_w_out):
    batch, s, d = x.shape
    depth = w_ada.shape[0]
    assert batch == 1 and depth == 1
    n_heads = rpb.shape[1]
    d_attn = n_heads * HEAD_DIM
    d_conv = conv_w.shape[-1]
    assert d_attn == 1024 and d_conv == 1024
    rows = s // GRID_W

    xs = x[0]
    cs = ctx[0]
    l = 0

    mods = _ada_call(jnp.stack([c[0], c_ctx], axis=1), w_ada[l], b_ada[l])
    mx = mods[0].reshape(N_MOD, d)
    mc = mods[1].reshape(N_MOD, d)

    w1_in = ff1_w_in[l].astype(bf16)
    w1_out = ff1_w_out[l].astype(bf16)
    w2_in = ff2_w_in[l].astype(bf16)
    w2_out = ff2_w_out[l].astype(bf16)
    w_mix = w_in[l].astype(bf16)
    w_o = w_out[l].astype(bf16)

    rope = _rope_tables(rows)

    xs = _ffn_call(xs, mx[0:3], ff1_norm[l], w1_in, w1_out, tm=1024, tf=512)
    cs = _ffn_call(cs, mc[0:3], ff1_norm[l], w1_in, w1_out, tm=cs.shape[0], tf=512)

    mixed = _proj_call(xs, mx[3:5], mix_norm[l], w_mix, q_norm[l], k_norm[l], rope,
                       kinds=("q", "k", "plain", "plain", "mul_lo", "mul_hi"),
                       w_blocks=((0, 1), (2, 3), (4, 5), (6, 7), (8, 10), (9, 11)),
                       out_cols=(0, 1, 2, 3, 4, 4), tm=1024)
    kvc = _proj_call(cs, mc[3:5], mix_norm[l], w_mix, q_norm[l], k_norm[l], None,
                     kinds=("k_nopos", "plain"), w_blocks=((2, 3), (4, 5)), out_cols=(0, 1),
                     tm=cs.shape[0])

    attn = _attn_call(_logits_bounded(q_norm[l], k_norm[l], rpb[l]), mixed, kvc, _rpb_lanes(rpb[l]), n_heads, rows)
    xs = _out_call(attn, mixed, xs, conv_w[l], conv_b[l], out_norm_attn[l], out_norm_conv[l], mx[5], w_o,
                   tm=512, bg_col=3, z_col=4)

    xs = _ffn_call(xs, mx[6:9], ff2_norm[l], w2_in, w2_out, tm=1024, tf=512)
    return xs[None]
```

```python
import functools

import numpy as np
import jax
import jax.numpy as jnp
from jax import lax
from jax.experimental import pallas as pl
from jax.experimental.pallas import tpu as pltpu

GRID_W = 64
HEAD_DIM = 64
WIN_H = 8
WIN_W = 16
ROPE_BASE = 10000.0
EPS = 1e-6
N_MOD = 9

LANES = 128
HEADS_PER_SLAB = LANES // HEAD_DIM
VMEM_LIMIT = 60 * 1024 * 1024
NEG = -1e30

Q_ROWS = 8
SUB_ROWS = 4
K_ROWS = SUB_ROWS + WIN_H
N_TILES = 2 * WIN_H
NO_TILE = N_TILES - 1
SOFTMAX_ROWS = 16
LOGIT_BOUND = 20.0

f32 = jnp.float32
bf16 = jnp.bfloat16


def _params(*sem, flags=None):
    return pltpu.CompilerParams(dimension_semantics=sem, vmem_limit_bytes=VMEM_LIMIT, flags=flags)


def _pick(n, values):
    out = jnp.int32(values[-1])
    for idx in range(len(values) - 2, -1, -1):
        out = jnp.where(n == idx, values[idx], out)
    return out


def _ada_kernel(c_ref, w_ref, b_ref, o_ref):
    c = c_ref[...]
    s = c * jax.nn.sigmoid(c)
    w = w_ref[...]
    r0 = jnp.sum(w * s[:, 0:1], axis=0, keepdims=True)
    r1 = jnp.sum(w * s[:, 1:2], axis=0, keepdims=True)
    o_ref[...] = jnp.concatenate([r0, r1], axis=0) + b_ref[...]


def _ada_call(cvecs_t, w_ada, b_ada, tn=1024):
    d, n = w_ada.shape
    return pl.pallas_call(
        _ada_kernel,
        grid=(n // tn,),
        in_specs=[
            pl.BlockSpec((d, 2), lambda j: (0, 0)),
            pl.BlockSpec((d, tn), lambda j: (0, j)),
            pl.BlockSpec((1, tn), lambda j: (0, j)),
        ],
        out_specs=pl.BlockSpec((2, tn), lambda j: (0, j)),
        out_shape=jax.ShapeDtypeStruct((2, n), f32),
        compiler_params=_params("arbitrary"),
        name="ada",
    )(cvecs_t, w_ada, b_ada.reshape(1, n))


def _norm_mod_into(x_ref, g_ref, mod_ref, h_ref, inv_ref, copy_ref=None, rows_per_chunk=256):
    tm, d = x_ref.shape
    rc = min(rows_per_chunk, tm)

    def inv_rms(r, carry):
        rows = pl.ds(pl.multiple_of(r * rc, rc), rc)
        x = x_ref[rows, :]
        ms = jnp.sum(x * x, axis=-1, keepdims=True) * (1.0 / d)
        inv_ref[rows, :] = jnp.broadcast_to(lax.rsqrt(ms + EPS), (rc, LANES))
        return carry

    def scale(r, carry):
        rows = pl.ds(pl.multiple_of(r * rc, rc), rc)
        inv = inv_ref[rows, :]
        for c in range(d // LANES):
            cols = slice(c * LANES, (c + 1) * LANES)
            x = x_ref[rows, cols]
            y = x * inv * g_ref[:, cols]
            h_ref[rows, cols] = (y * (1.0 + mod_ref[1:2, cols]) + mod_ref[0:1, cols]).astype(bf16)
            if copy_ref is not None:
                copy_ref[rows, cols] = x
        return carry

    lax.fori_loop(0, tm // rc, inv_rms, 0)
    lax.fori_loop(0, tm // rc, scale, 0)


def _ffn_kernel(n_split, x_ref, mod_ref, g_ref, wa_ref, wb_ref, wo_ref, o_ref, h_ref, inv_ref):
    @pl.when(pl.program_id(1) == 0)
    def _():
        _norm_mod_into(x_ref, g_ref, mod_ref, h_ref, inv_ref, copy_ref=o_ref)

    h = h_ref[...]
    a = jnp.dot(h, wa_ref[...], preferred_element_type=f32)
    b = jnp.dot(h, wb_ref[...], preferred_element_type=f32)
    act = (a * jax.nn.sigmoid(a) * b).astype(bf16)
    half_gate = 0.5 * mod_ref[2:3, :]
    w = o_ref.shape[1] // n_split
    for c in range(n_split):
        cols = slice(c * w, (c + 1) * w)
        o_ref[:, cols] += half_gate[:, cols] * jnp.dot(act, wo_ref[:, cols], preferred_element_type=f32)


def _ffn_call(x, mods, g, w_in, w_out, tm, tf, n_split=4):
    m, d = x.shape
    ff = w_out.shape[0]
    nf = ff // tf
    return pl.pallas_call(
        functools.partial(_ffn_kernel, n_split),
        grid=(m // tm, nf),
        in_specs=[
            pl.BlockSpec((tm, d), lambda i, j: (i, 0)),
            pl.BlockSpec((3, d), lambda i, j: (0, 0)),
            pl.BlockSpec((1, d), lambda i, j: (0, 0)),
            pl.BlockSpec((d, tf), lambda i, j: (0, j)),
            pl.BlockSpec((d, tf), lambda i, j: (0, j + nf)),
            pl.BlockSpec((tf, d), lambda i, j: (j, 0)),
        ],
        out_specs=pl.BlockSpec((tm, d), lambda i, j: (i, 0)),
        out_shape=jax.ShapeDtypeStruct((m, d), f32),
        scratch_shapes=[pltpu.VMEM((tm, d), bf16), pltpu.VMEM((tm, LANES), f32)],
        compiler_params=_params("arbitrary", "arbitrary"),
        name="ffn",
    )(x, mods, g.reshape(1, d), w_in, w_in, w_out)


def _proj_kernel(kinds, *refs):
    use_rope = any(k in ("q", "k") for k in kinds)
    if use_rope:
        (x_ref, mod_ref, g_ref, wa_ref, wb_ref, qg_ref, kg_ref, rrow_ref, rcol_ref,
         o_ref, h_ref, inv_ref, cos_ref, sin_ref) = refs
    else:
        x_ref, mod_ref, g_ref, wa_ref, wb_ref, qg_ref, kg_ref, o_ref, h_ref, inv_ref = refs
    i = pl.program_id(0)
    n = pl.program_id(1)
    tm = o_ref.shape[0]
    tw = wa_ref.shape[1]

    @pl.when(n == 0)
    def _():
        _norm_mod_into(x_ref, g_ref, mod_ref, h_ref, inv_ref)
        if use_rope:
            for r in range(tm // GRID_W):
                tok = slice(r * GRID_W, (r + 1) * GRID_W)
                grid_row = i * (tm // GRID_W) + r
                cos_ref[tok, :] = rrow_ref[0, pl.ds(grid_row, 1), :] + rcol_ref[0]
                sin_ref[tok, :] = rrow_ref[1, pl.ds(grid_row, 1), :] + rcol_ref[1]

    h = h_ref[...]
    res = (jnp.dot(h, wa_ref[...], preferred_element_type=f32),
           jnp.dot(h, wb_ref[...], preferred_element_type=f32))

    pair = 2 * LANES
    row = lax.broadcasted_iota(jnp.int32, (pair, pair), 0) // HEAD_DIM
    col = lax.broadcasted_iota(jnp.int32, (pair, pair), 1) // HEAD_DIM
    same_head = (row == col).astype(bf16)
    lane = lax.broadcasted_iota(jnp.int32, (1, LANES), 1)
    quarter = HEAD_DIM // 4
    upper = (lane & quarter) != 0

    def headwise(gain_ref, rope):
        gain = gain_ref[...]
        for part in range(2):
            for s in range(tw // pair):
                y = res[part][:, s * pair:(s + 1) * pair]
                ssq = jnp.dot((y * y).astype(bf16), same_head, preferred_element_type=f32)
                y = y * lax.rsqrt(ssq * (1.0 / HEAD_DIM) + EPS) * gain
                for u in range(2):
                    yu = y[:, u * LANES:(u + 1) * LANES]
                    if rope:
                        partner = jnp.where(upper, pltpu.roll(yu, quarter, 1), pltpu.roll(yu, LANES - quarter, 1))
                        yu = yu * cos_ref[...] + partner * sin_ref[...]
                    lo = part * tw + s * pair + u * LANES
                    o_ref[:, lo:lo + LANES] = yu.astype(o_ref.dtype)

    for idx, kind in enumerate(kinds):
        @pl.when(n == idx)
        def _(kind=kind):
            if kind == "q":
                headwise(qg_ref, True)
            elif kind == "k":
                headwise(kg_ref, True)
            elif kind == "k_nopos":
                headwise(kg_ref, False)
            elif kind == "plain":
                o_ref[:, :tw] = res[0].astype(o_ref.dtype)
                o_ref[:, tw:] = res[1].astype(o_ref.dtype)
            elif kind == "mul_lo":
                o_ref[:, :tw] = (res[0] * res[1]).astype(o_ref.dtype)
            elif kind == "mul_hi":
                o_ref[:, tw:] = (res[0] * res[1]).astype(o_ref.dtype)
            else:
                raise ValueError(kind)


def _proj_call(x, mods, g, w, qg, kg, rope, kinds, w_blocks, out_cols, tm, tn=1024):
    m, d = x.shape
    tw = tn // 2
    use_rope = rope is not None
    n_out = max(out_cols) + 1
    blk_a = tuple(b[0] for b in w_blocks)
    blk_b = tuple(b[1] for b in w_blocks)
    out_cols = tuple(out_cols)
    pair_gain = lambda v: jnp.tile(v, 2 * HEADS_PER_SLAB)[None, :]
    assert tm % GRID_W == 0

    in_specs = [
        pl.BlockSpec((tm, d), lambda i, n: (i, 0)),
        pl.BlockSpec((2, d), lambda i, n: (0, 0)),
        pl.BlockSpec((1, d), lambda i, n: (0, 0)),
        pl.BlockSpec((d, tw), lambda i, n: (0, _pick(n, blk_a))),
        pl.BlockSpec((d, tw), lambda i, n: (0, _pick(n, blk_b))),
        pl.BlockSpec((1, 2 * LANES), lambda i, n: (0, 0)),
        pl.BlockSpec((1, 2 * LANES), lambda i, n: (0, 0)),
    ]
    args = [x, mods, g.reshape(1, d), w, w, pair_gain(qg), pair_gain(kg)]
    scratch = [pltpu.VMEM((tm, d), bf16), pltpu.VMEM((tm, LANES), f32)]
    if use_rope:
        in_specs += [pl.BlockSpec(t.shape, lambda i, n: (0, 0, 0)) for t in rope]
        args += list(rope)
        scratch += [pltpu.VMEM((tm, LANES), f32)] * 2
    return pl.pallas_call(
        functools.partial(_proj_kernel, tuple(kinds)),
        grid=(m // tm, len(kinds)),
        in_specs=in_specs,
        out_specs=pl.BlockSpec((tm, tn), lambda i, n: (i, _pick(n, out_cols))),
        out_shape=jax.ShapeDtypeStruct((m, n_out * tn), bf16),
        scratch_shapes=scratch,
        compiler_params=_params("arbitrary", "arbitrary"),
        name="proj",
    )(*args)


def _row_window_tables(rows):
    tables = []
    for r0 in (0, SUB_ROWS, rows - SUB_ROWS):
        ks = min(max(r0 - WIN_H // 2, 0), rows - K_ROWS)
        tab = [[NO_TILE] * K_ROWS for _ in range(SUB_ROWS)]
        for a in range(SUB_ROWS):
            r = r0 + a
            rs = min(max(r - WIN_H // 2, 0), rows - WIN_H)
            assert ks <= rs and rs + WIN_H <= ks + K_ROWS
            for b in range(K_ROWS):
                kr = ks + b
                if rs <= kr < rs + WIN_H:
                    tab[a][b] = kr - r + (WIN_H - 1)
        tables.append(tab)
    return tables


def _attn_kernel(rows, bounded_ref, q_ref, k_ref, v_ref, kc_ref, vc_ref, rpb_ref, o_ref,
                 tile_ref, bias_ref, kall_ref, vall_ref, s_ref, p_ref, m_ref):
    i = pl.program_id(1)
    nblk = pl.num_programs(1)
    tq = SUB_ROWS * GRID_W
    tk = K_ROWS * GRID_W
    lc = kc_ref.shape[0]
    key_tile = 2 * LANES
    lane = lax.broadcasted_iota(jnp.int32, (1, LANES), 1)
    first_head = lane < HEAD_DIM
    ones_tile = jnp.ones((key_tile, LANES), bf16)
    nt = (((1,), (1,)), ((), ()))

    def build_tiles():
        qc = lax.broadcasted_iota(jnp.int32, (GRID_W, LANES), 0)
        kc = lax.broadcasted_iota(jnp.int32, (GRID_W, LANES), 1) % GRID_W
        cs = jnp.clip(qc - WIN_W // 2, 0, GRID_W - WIN_W)
        in_window = (kc >= cs) & (kc < cs + WIN_W)
        for hh in range(HEADS_PER_SLAB):
            for e in range(NO_TILE):
                base = jnp.broadcast_to(rpb_ref[hh, e:e + 1, :], (GRID_W, LANES))
                shifted = pltpu.roll(base, LANES - (WIN_W - 1), 1, stride=1, stride_axis=0)
                tile_ref[hh, e] = jnp.where(in_window, shifted, NEG)
            tile_ref[hh, NO_TILE] = jnp.full((GRID_W, LANES), NEG, f32)

    def assemble(variant, tab):
        for hh in range(HEADS_PER_SLAB):
            for a in range(SUB_ROWS):
                for bp in range(K_ROWS // 2):
                    e0, e1 = tab[a][2 * bp], tab[a][2 * bp + 1]
                    if e0 == e1:
                        tile = tile_ref[hh, e0]
                    else:
                        tile = jnp.where(first_head, tile_ref[hh, e0], tile_ref[hh, e1])
                    lo = hh * tq + a * GRID_W
                    bias_ref[variant, lo:lo + GRID_W, bp * LANES:(bp + 1) * LANES] = tile

    @pl.when(i == 0)
    def _():
        build_tiles()
        for variant, tab in enumerate(_row_window_tables(rows)):
            assemble(variant, tab)
        kall_ref[tk:, :] = kc_ref[...]
        vall_ref[tk:, 0:LANES] = vc_ref[...]
        vall_ref[:, LANES:] = jnp.ones((tk + lc, LANES), bf16)

    def sub_block(sb):
        r0 = i * Q_ROWS + sb * SUB_ROWS
        ks = jnp.clip(r0 - WIN_H // 2, 0, rows - K_ROWS) * GRID_W
        ks = pl.multiple_of(ks, SUB_ROWS * GRID_W)
        if sb == 0:
            variant = jnp.where(i == 0, 0, 1)
        elif sb == Q_ROWS // SUB_ROWS - 1:
            variant = jnp.where(i == nblk - 1, 2, 1)
        else:
            variant = 1
        q = q_ref[sb * tq:(sb + 1) * tq, :] * jnp.asarray(HEAD_DIM ** -0.5, bf16)
        zero = jnp.zeros_like(q)
        q_stack = jnp.concatenate([jnp.where(first_head, q, zero), jnp.where(first_head, zero, q)], axis=0)
        return ks, variant, q_stack

    def write_out(sb, prod):
        o = prod[:, :LANES] * (1.0 / prod[:, LANES:])
        o_ref[sb * tq:(sb + 1) * tq, :] = jnp.where(first_head, o[:tq], o[tq:]).astype(o_ref.dtype)

    @pl.when(bounded_ref[0] != 0)
    def _():
        for sb in range(Q_ROWS // SUB_ROWS):
            ks, variant, q_stack = sub_block(sb)
            acc = None
            for lo in range(0, tk + lc, key_tile):
                if lo < tk:
                    keys = pl.ds(ks + lo, key_tile)
                    k_tile, v_tile = k_ref[keys, :], v_ref[keys, :]
                else:
                    k_tile, v_tile = kc_ref[lo - tk:lo - tk + key_tile, :], vc_ref[lo - tk:lo - tk + key_tile, :]
                t = lax.dot_general(q_stack, k_tile, nt, preferred_element_type=f32)
                if lo < tk:
                    t = t + bias_ref[variant, :, lo:lo + key_tile]
                d = jnp.dot(jnp.exp(t).astype(bf16), jnp.concatenate([v_tile, ones_tile], axis=1),
                            preferred_element_type=f32)
                acc = d if acc is None else acc + d
            write_out(sb, acc)

    @pl.when(bounded_ref[0] == 0)
    def _():
        n_loc = tk // LANES
        n_all = (tk + lc) // LANES
        groups = [slice(r * SOFTMAX_ROWS, (r + 1) * SOFTMAX_ROWS) for r in range(2 * tq // SOFTMAX_ROWS)]
        for sb in range(Q_ROWS // SUB_ROWS):
            ks, variant, q_stack = sub_block(sb)

            def logits(rws, c, minus=None):
                cols = slice(c * LANES, (c + 1) * LANES)
                t = s_ref[rws, cols]
                if minus is not None:
                    t = t - minus
                return t + bias_ref[variant, rws, cols] if c < n_loc else t

            kall_ref[0:tk, :] = k_ref[pl.ds(ks, tk), :]
            vall_ref[0:tk, 0:LANES] = v_ref[pl.ds(ks, tk), :]
            s_ref[...] = lax.dot_general(q_stack, kall_ref[...], nt, preferred_element_type=f32)
            for rws in groups:
                acc = logits(rws, 0)
                for c in range(1, n_all):
                    acc = jnp.maximum(acc, logits(rws, c))
                m_ref[rws, :] = jnp.broadcast_to(jnp.max(acc, axis=-1, keepdims=True), (SOFTMAX_ROWS, LANES))
            for rws in groups:
                m = m_ref[rws, :]
                for c in range(n_all):
                    p_ref[rws, c * LANES:(c + 1) * LANES] = jnp.exp(logits(rws, c, minus=m)).astype(bf16)
            write_out(sb, jnp.dot(p_ref[...], vall_ref[...], preferred_element_type=f32))


def _attn_call(bounded, qkv, kvc, rpb_lanes, n_heads, rows):
    s = qkv.shape[0]
    d_attn = n_heads * HEAD_DIM
    slabs = d_attn // LANES
    tq = SUB_ROWS * GRID_W
    tk = K_ROWS * GRID_W
    nblk = rows // Q_ROWS
    assert rows % Q_ROWS == 0 and Q_ROWS % SUB_ROWS == 0 and rows >= K_ROWS + SUB_ROWS
    assert SUB_ROWS % (WIN_H // 2) == 0
    lc = kvc.shape[0]
    assert tk % (2 * LANES) == 0 and lc % (2 * LANES) == 0
    grid_spec = pltpu.PrefetchScalarGridSpec(
        num_scalar_prefetch=1,
        grid=(slabs, nblk),
        in_specs=[
            pl.BlockSpec((Q_ROWS * GRID_W, LANES), lambda h, i, b: (i, h)),
            pl.BlockSpec((s, LANES), lambda h, i, b: (0, slabs + h)),
            pl.BlockSpec((s, LANES), lambda h, i, b: (0, 2 * slabs + h)),
            pl.BlockSpec((lc, LANES), lambda h, i, b: (0, h)),
            pl.BlockSpec((lc, LANES), lambda h, i, b: (0, slabs + h)),
            pl.BlockSpec((HEADS_PER_SLAB, N_TILES, LANES), lambda h, i, b: (h, 0, 0)),
        ],
        out_specs=pl.BlockSpec((Q_ROWS * GRID_W, LANES), lambda h, i, b: (i, h)),
        scratch_shapes=[
            pltpu.VMEM((HEADS_PER_SLAB, N_TILES, GRID_W, LANES), f32),
            pltpu.VMEM((3, HEADS_PER_SLAB * tq, tk), f32),
            pltpu.VMEM((tk + lc, LANES), bf16),
            pltpu.VMEM((tk + lc, 2 * LANES), bf16),
            pltpu.VMEM((HEADS_PER_SLAB * tq, tk + lc), f32),
            pltpu.VMEM((HEADS_PER_SLAB * tq, tk + lc), bf16),
            pltpu.VMEM((HEADS_PER_SLAB * tq, LANES), f32),
        ],
    )
    return pl.pallas_call(
        functools.partial(_attn_kernel, rows),
        grid_spec=grid_spec,
        out_shape=jax.ShapeDtypeStruct((s, d_attn), bf16),
        compiler_params=_params("arbitrary", "arbitrary"),
        name="attn",
    )(bounded, qkv, qkv, qkv, kvc, kvc, rpb_lanes)


def _rms(x, g):
    ms = jnp.mean(x * x, axis=-1, keepdims=True)
    return x * lax.rsqrt(ms + EPS) * g


def _out_kernel(attn_ref, bg_ref, z_ref, zp_ref, zn_ref, x_ref, cw_ref, cb_ref, ga_ref, gc_ref, gate_ref,
                w_ref, o_ref):
    i = pl.program_id(0)
    nt = pl.num_programs(0)
    tm, dc = z_ref.shape
    da = attn_ref.shape[1]
    z = z_ref[...].astype(f32)
    halo = zp_ref.shape[0]
    before = jnp.where(i > 0, zp_ref[halo - 1:halo, :].astype(f32), 0.0)
    after = jnp.where(i < nt - 1, zn_ref[0:1, :].astype(f32), 0.0)
    row = lax.broadcasted_iota(jnp.int32, (tm, 1), 0)
    z_m = jnp.where(row == 0, before, pltpu.roll(z, 1, 0))
    z_p = jnp.where(row == tm - 1, after, pltpu.roll(z, tm - 1, 0))
    y = cb_ref[...] + z_m * cw_ref[0:1, :]
    y = y + z * cw_ref[1:2, :]
    y = y + z_p * cw_ref[2:3, :]
    conv = bg_ref[...].astype(f32) * y
    cn = _rms(conv, gc_ref[...]).astype(bf16)
    an = _rms(attn_ref[...].astype(f32), ga_ref[...]).astype(bf16)
    proj = (jnp.dot(an, w_ref[0:da, :], preferred_element_type=f32)
            + jnp.dot(cn, w_ref[da:da + dc, :], preferred_element_type=f32))
    o_ref[...] = x_ref[...] + gate_ref[...] * proj


def _out_call(attn, mixed, x, conv_w, conv_b, ga, gc, gate, w_out, tm, bg_col, z_col, halo=16):
    s, d = x.shape
    da = attn.shape[1]
    dc = conv_w.shape[1]
    per = tm // halo
    nh = s // halo
    return pl.pallas_call(
        _out_kernel,
        grid=(s // tm,),
        in_specs=[
            pl.BlockSpec((tm, da), lambda i: (i, 0)),
            pl.BlockSpec((tm, dc), lambda i: (i, bg_col)),
            pl.BlockSpec((tm, dc), lambda i: (i, z_col)),
            pl.BlockSpec((halo, dc), lambda i: (jnp.maximum(i * per - 1, 0), z_col)),
            pl.BlockSpec((halo, dc), lambda i: (jnp.minimum((i + 1) * per, nh - 1), z_col)),
            pl.BlockSpec((tm, d), lambda i: (i, 0)),
            pl.BlockSpec((3, dc), lambda i: (0, 0)),
            pl.BlockSpec((1, dc), lambda i: (0, 0)),
            pl.BlockSpec((1, da), lambda i: (0, 0)),
            pl.BlockSpec((1, dc), lambda i: (0, 0)),
            pl.BlockSpec((1, d), lambda i: (0, 0)),
            pl.BlockSpec((da + dc, d), lambda i: (0, 0)),
        ],
        out_specs=pl.BlockSpec((tm, d), lambda i: (i, 0)),
        out_shape=jax.ShapeDtypeStruct((s, d), f32),
        compiler_params=_params("arbitrary"),
        name="out",
    )(attn, mixed, mixed, mixed, mixed, x, conv_w, conv_b.reshape(1, dc), ga.reshape(1, da),
      gc.reshape(1, dc), gate.reshape(1, d), w_out)


def _rope_tables(rows):
    nf = HEAD_DIM // 4
    inv = ROPE_BASE ** (-jnp.arange(nf, dtype=f32) / nf)
    zero = jnp.zeros((1, nf), f32)

    def lanes(pos, first_half):
        ang = jnp.arange(pos, dtype=jnp.int32).astype(f32)[:, None] * inv[None, :]
        cos, sin = jnp.cos(ang), jnp.sin(ang)
        z = jnp.broadcast_to(zero, cos.shape)
        cos_head = [cos, cos, z, z] if first_half else [z, z, cos, cos]
        sin_head = [-sin, sin, z, z] if first_half else [z, z, -sin, sin]
        table = lambda parts: jnp.tile(jnp.concatenate(parts, axis=-1), (1, HEADS_PER_SLAB))
        return jnp.stack([table(cos_head), table(sin_head)])

    return lanes(rows, True), lanes(GRID_W, False)


def _logits_bounded(q_gain, k_gain, rpb):
    bound = (HEAD_DIM ** 0.5) * 1.02 * jnp.max(jnp.abs(q_gain)) * jnp.max(jnp.abs(k_gain)) + jnp.max(jnp.abs(rpb))
    return (bound <= LOGIT_BOUND).astype(jnp.int32).reshape(1)


def _rpb_lanes(rpb):
    h, nr, ncol = rpb.shape
    half = jnp.pad(rpb, ((0, 0), (0, N_TILES - nr), (0, GRID_W - ncol)))
    return jnp.concatenate([half, half], axis=-1)


def kernel(x, c, ctx, c_ctx, w_ada, b_ada, ff1_norm, ff1_w_in, ff1_w_out, mix_norm, w_in, q_norm, k_norm,
           rpb, conv_w, conv_b, out_norm_attn, out_norm_conv, w_out, ff2_norm, ff2_w_in, ff2_w_out):
    batch, s, d = x.shape
    depth = w_ada.shape[0]
    assert batch == 1 and depth == 1
    n_heads = rpb.shape[1]
    d_attn = n_heads * HEAD_DIM
    d_conv = conv_w.shape[-1]
    assert d_attn == 1024 and d_conv == 1024
    rows = s // GRID_W

    xs = x[0]
    cs = ctx[0]
    l = 0

    mods = _ada_call(jnp.stack([c[0], c_ctx], axis=1), w_ada[l], b_ada[l])
    mx = mods[0].reshape(N_MOD, d)
    mc = mods[1].reshape(N_MOD, d)

    w1_in = ff1_w_in[l].astype(bf16)
    w1_out = ff1_w_out[l].astype(bf16)
    w2_in = ff2_w_in[l].astype(bf16)
    w2_out = ff2_w_out[l].astype(bf16)
    w_mix = w_in[l].astype(bf16)
    w_o = w_out[l].astype(bf16)

    rope = _rope_tables(rows)

    xs = _ffn_call(xs, mx[0:3], ff1_norm[l], w1_in, w1_out, tm=1024, tf=512)
    cs = _ffn_call(cs, mc[0:3], ff1_norm[l], w1_in, w1_out, tm=cs.shape[0], tf=512)

    mixed = _proj_call(xs, mx[3:5], mix_norm[l], w_mix, q_norm[l], k_norm[l], rope,
                       kinds=("q", "k", "plain", "plain", "mul_lo", "mul_hi"),
                       w_blocks=((0, 1), (2, 3), (4, 5), (6, 7), (8, 10), (9, 11)),
                       out_cols=(0, 1, 2, 3, 4, 4), tm=1024)
    kvc = _proj_call(cs, mc[3:5], mix_norm[l], w_mix, q_norm[l], k_norm[l], None,
                     kinds=("k_nopos", "plain"), w_blocks=((2, 3), (4, 5)), out_cols=(0, 1),
                     tm=cs.shape[0])

    attn = _attn_call(_logits_bounded(q_norm[l], k_norm[l], rpb[l]), mixed, kvc, _rpb_lanes(rpb[l]), n_heads, rows)
    xs = _out_call(attn, mixed, xs, conv_w[l], conv_b[l], out_norm_attn[l], out_norm_conv[l], mx[5], w_o,
                   tm=512, bg_col=3, z_col=4)

    xs = _ffn_call(xs, mx[6:9], ff2_norm[l], w2_in, w2_out, tm=1024, tf=512)
    return xs[None]
```

```python
import functools

import numpy as np
import jax
import jax.numpy as jnp
from jax import lax
from jax.experimental import pallas as pl
from jax.experimental.pallas import tpu as pltpu

GRID_W = 64
HEAD_DIM = 64
WIN_H = 8
WIN_W = 16
ROPE_BASE = 10000.0
EPS = 1e-6
N_MOD = 9

LANES = 128
HEADS_PER_SLAB = LANES // HEAD_DIM
VMEM_LIMIT = 60 * 1024 * 1024
NEG = -1e30

Q_ROWS = 32
SUB_ROWS = 4
K_ROWS = SUB_ROWS + WIN_H
N_TILES = 2 * WIN_H
NO_TILE = N_TILES - 1
SOFTMAX_ROWS = 16
LOGIT_BOUND = 20.0

f32 = jnp.float32
bf16 = jnp.bfloat16


def _params(*sem, flags=None):
    return pltpu.CompilerParams(dimension_semantics=sem, vmem_limit_bytes=VMEM_LIMIT, flags=flags)


def _pick(n, values):
    out = jnp.int32(values[-1])
    for idx in range(len(values) - 2, -1, -1):
        out = jnp.where(n == idx, values[idx], out)
    return out


def _ada_kernel(c_ref, w_ref, b_ref, o_ref):
    c = c_ref[...]
    s = c * jax.nn.sigmoid(c)
    w = w_ref[...]
    r0 = jnp.sum(w * s[:, 0:1], axis=0, keepdims=True)
    r1 = jnp.sum(w * s[:, 1:2], axis=0, keepdims=True)
    o_ref[...] = jnp.concatenate([r0, r1], axis=0) + b_ref[...]


def _ada_call(cvecs_t, w_ada, b_ada, tn=2048):
    d, n = w_ada.shape
    return pl.pallas_call(
        _ada_kernel,
        grid=(n // tn,),
        in_specs=[
            pl.BlockSpec((d, 2), lambda j: (0, 0)),
            pl.BlockSpec((d, tn), lambda j: (0, j)),
            pl.BlockSpec((1, tn), lambda j: (0, j)),
        ],
        out_specs=pl.BlockSpec((2, tn), lambda j: (0, j)),
        out_shape=jax.ShapeDtypeStruct((2, n), f32),
        compiler_params=_params("arbitrary"),
        name="ada",
    )(cvecs_t, w_ada, b_ada.reshape(1, n))


def _norm_mod_into(x_ref, g_ref, mod_ref, h_ref, inv_ref, copy_ref=None, rows_per_chunk=256):
    tm, d = x_ref.shape
    rc = min(rows_per_chunk, tm)

    def inv_rms(r, carry):
        rows = pl.ds(pl.multiple_of(r * rc, rc), rc)
        x = x_ref[rows, :]
        ms = jnp.sum(x * x, axis=-1, keepdims=True) * (1.0 / d)
        inv_ref[rows, :] = jnp.broadcast_to(lax.rsqrt(ms + EPS), (rc, LANES))
        return carry

    def scale(r, carry):
        rows = pl.ds(pl.multiple_of(r * rc, rc), rc)
        inv = inv_ref[rows, :]
        for c in range(d // LANES):
            cols = slice(c * LANES, (c + 1) * LANES)
            x = x_ref[rows, cols]
            y = x * inv * g_ref[:, cols]
            h_ref[rows, cols] = (y * (1.0 + mod_ref[1:2, cols]) + mod_ref[0:1, cols]).astype(bf16)
            if copy_ref is not None:
                copy_ref[rows, cols] = x
        return carry

    lax.fori_loop(0, tm // rc, inv_rms, 0)
    lax.fori_loop(0, tm // rc, scale, 0)


def _ffn_kernel(n_split, x_ref, mod_ref, g_ref, wa_ref, wb_ref, wo_ref, o_ref, h_ref, inv_ref):
    @pl.when(pl.program_id(1) == 0)
    def _():
        _norm_mod_into(x_ref, g_ref, mod_ref, h_ref, inv_ref, copy_ref=o_ref)

    h = h_ref[...]
    a = jnp.dot(h, wa_ref[...], preferred_element_type=f32)
    b = jnp.dot(h, wb_ref[...], preferred_element_type=f32)
    act = (a * jax.nn.sigmoid(a) * b).astype(bf16)
    half_gate = 0.5 * mod_ref[2:3, :]
    w = o_ref.shape[1] // n_split
    for c in range(n_split):
        cols = slice(c * w, (c + 1) * w)
        o_ref[:, cols] += half_gate[:, cols] * jnp.dot(act, wo_ref[:, cols], preferred_element_type=f32)


def _ffn_call(x, mods, g, w_in, w_out, tm, tf, n_split=4):
    m, d = x.shape
    ff = w_out.shape[0]
    nf = ff // tf
    return pl.pallas_call(
        functools.partial(_ffn_kernel, n_split),
        grid=(m // tm, nf),
        in_specs=[
            pl.BlockSpec((tm, d), lambda i, j: (i, 0)),
            pl.BlockSpec((3, d), lambda i, j: (0, 0)),
            pl.BlockSpec((1, d), lambda i, j: (0, 0)),
            pl.BlockSpec((d, tf), lambda i, j: (0, j)),
            pl.BlockSpec((d, tf), lambda i, j: (0, j + nf)),
            pl.BlockSpec((tf, d), lambda i, j: (j, 0)),
        ],
        out_specs=pl.BlockSpec((tm, d), lambda i, j: (i, 0)),
        out_shape=jax.ShapeDtypeStruct((m, d), f32),
        scratch_shapes=[pltpu.VMEM((tm, d), bf16), pltpu.VMEM((tm, LANES), f32)],
        compiler_params=_params("arbitrary", "arbitrary"),
        name="ffn",
    )(x, mods, g.reshape(1, d), w_in, w_in, w_out)


def _proj_kernel(kinds, *refs):
    use_rope = any(k in ("q", "k") for k in kinds)
    if use_rope:
        (x_ref, mod_ref, g_ref, wa_ref, wb_ref, qg_ref, kg_ref, rrow_ref, rcol_ref,
         o_ref, h_ref, inv_ref, cos_ref, sin_ref) = refs
    else:
        x_ref, mod_ref, g_ref, wa_ref, wb_ref, qg_ref, kg_ref, o_ref, h_ref, inv_ref = refs
    i = pl.program_id(0)
    n = pl.program_id(1)
    tm = o_ref.shape[0]
    tw = wa_ref.shape[1]

    @pl.when(n == 0)
    def _():
        _norm_mod_into(x_ref, g_ref, mod_ref, h_ref, inv_ref)
        if use_rope:
            for r in range(tm // GRID_W):
                tok = slice(r * GRID_W, (r + 1) * GRID_W)
                grid_row = i * (tm // GRID_W) + r
                cos_ref[tok, :] = rrow_ref[0, pl.ds(grid_row, 1), :] + rcol_ref[0]
                sin_ref[tok, :] = rrow_ref[1, pl.ds(grid_row, 1), :] + rcol_ref[1]

    h = h_ref[...]
    res = (jnp.dot(h, wa_ref[...], preferred_element_type=f32),
           jnp.dot(h, wb_ref[...], preferred_element_type=f32))

    pair = 2 * LANES
    row = lax.broadcasted_iota(jnp.int32, (pair, pair), 0) // HEAD_DIM
    col = lax.broadcasted_iota(jnp.int32, (pair, pair), 1) // HEAD_DIM
    same_head = (row == col).astype(bf16)
    lane = lax.broadcasted_iota(jnp.int32, (1, LANES), 1)
    quarter = HEAD_DIM // 4
    upper = (lane & quarter) != 0

    def headwise(gain_ref, rope):
        gain = gain_ref[...]
        for part in range(2):
            for s in range(tw // pair):
                y = res[part][:, s * pair:(s + 1) * pair]
                ssq = jnp.dot((y * y).astype(bf16), same_head, preferred_element_type=f32)
                y = y * lax.rsqrt(ssq * (1.0 / HEAD_DIM) + EPS) * gain
                for u in range(2):
                    yu = y[:, u * LANES:(u + 1) * LANES]
                    if rope:
                        partner = jnp.where(upper, pltpu.roll(yu, quarter, 1), pltpu.roll(yu, LANES - quarter, 1))
                        yu = yu * cos_ref[...] + partner * sin_ref[...]
                    lo = part * tw + s * pair + u * LANES
                    o_ref[:, lo:lo + LANES] = yu.astype(o_ref.dtype)

    for idx, kind in enumerate(kinds):
        @pl.when(n == idx)
        def _(kind=kind):
            if kind == "q":
                headwise(qg_ref, True)
            elif kind == "k":
                headwise(kg_ref, True)
            elif kind == "k_nopos":
                headwise(kg_ref, False)
            elif kind == "plain":
                o_ref[:, :tw] = res[0].astype(o_ref.dtype)
                o_ref[:, tw:] = res[1].astype(o_ref.dtype)
            elif kind == "mul_lo":
                o_ref[:, :tw] = (res[0] * res[1]).astype(o_ref.dtype)
            elif kind == "mul_hi":
                o_ref[:, tw:] = (res[0] * res[1]).astype(o_ref.dtype)
            else:
                raise ValueError(kind)


def _proj_call(x, mods, g, w, qg, kg, rope, kinds, w_blocks, out_cols, tm, tn=1024):
    m, d = x.shape
    tw = tn // 2
    use_rope = rope is not None
    n_out = max(out_cols) + 1
    blk_a = tuple(b[0] for b in w_blocks)
    blk_b = tuple(b[1] for b in w_blocks)
    out_cols = tuple(out_cols)
    pair_gain = lambda v: jnp.tile(v, 2 * HEADS_PER_SLAB)[None, :]
    assert tm % GRID_W == 0

    in_specs = [
        pl.BlockSpec((tm, d), lambda i, n: (i, 0)),
        pl.BlockSpec((2, d), lambda i, n: (0, 0)),
        pl.BlockSpec((1, d), lambda i, n: (0, 0)),
        pl.BlockSpec((d, tw), lambda i, n: (0, _pick(n, blk_a))),
        pl.BlockSpec((d, tw), lambda i, n: (0, _pick(n, blk_b))),
        pl.BlockSpec((1, 2 * LANES), lambda i, n: (0, 0)),
        pl.BlockSpec((1, 2 * LANES), lambda i, n: (0, 0)),
    ]
    args = [x, mods, g.reshape(1, d), w, w, pair_gain(qg), pair_gain(kg)]
    scratch = [pltpu.VMEM((tm, d), bf16), pltpu.VMEM((tm, LANES), f32)]
    if use_rope:
        in_specs += [pl.BlockSpec(t.shape, lambda i, n: (0, 0, 0)) for t in rope]
        args += list(rope)
        scratch += [pltpu.VMEM((tm, LANES), f32)] * 2
    return pl.pallas_call(
        functools.partial(_proj_kernel, tuple(kinds)),
        grid=(m // tm, len(kinds)),
        in_specs=in_specs,
        out_specs=pl.BlockSpec((tm, tn), lambda i, n: (i, _pick(n, out_cols))),
        out_shape=jax.ShapeDtypeStruct((m, n_out * tn), bf16),
        scratch_shapes=scratch,
        compiler_params=_params("arbitrary", "arbitrary"),
        name="proj",
    )(*args)


def _row_window_tables(rows):
    tables = []
    for r0 in (0, SUB_ROWS, rows - SUB_ROWS):
        ks = min(max(r0 - WIN_H // 2, 0), rows - K_ROWS)
        tab = [[NO_TILE] * K_ROWS for _ in range(SUB_ROWS)]
        for a in range(SUB_ROWS):
            r = r0 + a
            rs = min(max(r - WIN_H // 2, 0), rows - WIN_H)
            assert ks <= rs and rs + WIN_H <= ks + K_ROWS
            for b in range(K_ROWS):
                kr = ks + b
                if rs <= kr < rs + WIN_H:
                    tab[a][b] = kr - r + (WIN_H - 1)
        tables.append(tab)
    return tables


def _attn_kernel(rows, bounded_ref, q_ref, k_ref, v_ref, kc_ref, vc_ref, rpb_ref, o_ref,
                 tile_ref, bias_ref, kall_ref, vall_ref, s_ref, p_ref, m_ref):
    i = pl.program_id(1)
    nblk = pl.num_programs(1)
    tq = SUB_ROWS * GRID_W
    tk = K_ROWS * GRID_W
    lc = kc_ref.shape[0]
    key_tile = 2 * LANES
    lane = lax.broadcasted_iota(jnp.int32, (1, LANES), 1)
    first_head = lane < HEAD_DIM
    ones_tile = jnp.ones((key_tile, LANES), bf16)
    nt = (((1,), (1,)), ((), ()))

    def build_tiles():
        qc = lax.broadcasted_iota(jnp.int32, (GRID_W, LANES), 0)
        kc = lax.broadcasted_iota(jnp.int32, (GRID_W, LANES), 1) % GRID_W
        cs = jnp.clip(qc - WIN_W // 2, 0, GRID_W - WIN_W)
        in_window = (kc >= cs) & (kc < cs + WIN_W)
        for hh in range(HEADS_PER_SLAB):
            for e in range(NO_TILE):
                base = jnp.broadcast_to(rpb_ref[hh, e:e + 1, :], (GRID_W, LANES))
                shifted = pltpu.roll(base, LANES - (WIN_W - 1), 1, stride=1, stride_axis=0)
                tile_ref[hh, e] = jnp.where(in_window, shifted, NEG)
            tile_ref[hh, NO_TILE] = jnp.full((GRID_W, LANES), NEG, f32)

    def assemble(variant, tab):
        for hh in range(HEADS_PER_SLAB):
            for a in range(SUB_ROWS):
                for bp in range(K_ROWS // 2):
                    e0, e1 = tab[a][2 * bp], tab[a][2 * bp + 1]
                    if e0 == e1:
                        tile = tile_ref[hh, e0]
                    else:
                        tile = jnp.where(first_head, tile_ref[hh, e0], tile_ref[hh, e1])
                    lo = hh * tq + a * GRID_W
                    bias_ref[variant, lo:lo + GRID_W, bp * LANES:(bp + 1) * LANES] = tile

    @pl.when(i == 0)
    def _():
        build_tiles()
        for variant, tab in enumerate(_row_window_tables(rows)):
            assemble(variant, tab)
        kall_ref[tk:, :] = kc_ref[...]
        vall_ref[tk:, 0:LANES] = vc_ref[...]
        vall_ref[:, LANES:] = jnp.ones((tk + lc, LANES), bf16)

    def sub_block(sb):
        r0 = i * Q_ROWS + sb * SUB_ROWS
        ks = jnp.clip(r0 - WIN_H // 2, 0, rows - K_ROWS) * GRID_W
        ks = pl.multiple_of(ks, SUB_ROWS * GRID_W)
        if sb == 0:
            variant = jnp.where(i == 0, 0, 1)
        elif sb == Q_ROWS // SUB_ROWS - 1:
            variant = jnp.where(i == nblk - 1, 2, 1)
        else:
            variant = 1
        q = q_ref[sb * tq:(sb + 1) * tq, :] * jnp.asarray(HEAD_DIM ** -0.5, bf16)
        zero = jnp.zeros_like(q)
        q_stack = jnp.concatenate([jnp.where(first_head, q, zero), jnp.where(first_head, zero, q)], axis=0)
        return ks, variant, q_stack

    def write_out(sb, prod):
        o = prod[:, :LANES] * (1.0 / prod[:, LANES:])
        o_ref[sb * tq:(sb + 1) * tq, :] = jnp.where(first_head, o[:tq], o[tq:]).astype(o_ref.dtype)

    @pl.when(bounded_ref[0] != 0)
    def _():
        for sb in range(Q_ROWS // SUB_ROWS):
            ks, variant, q_stack = sub_block(sb)
            acc = None
            for lo in range(0, tk + lc, key_tile):
                if lo < tk:
                    keys = pl.ds(ks + lo, key_tile)
                    k_tile, v_tile = k_ref[keys, :], v_ref[keys, :]
                else:
                    k_tile, v_tile = kc_ref[lo - tk:lo - tk + key_tile, :], vc_ref[lo - tk:lo - tk + key_tile, :]
                t = lax.dot_general(q_stack, k_tile, nt, preferred_element_type=f32)
                if lo < tk:
                    t = t + bias_ref[variant, :, lo:lo + key_tile]
                d = jnp.dot(jnp.exp(t).astype(bf16), jnp.concatenate([v_tile, ones_tile], axis=1),
                            preferred_element_type=f32)
                acc = d if acc is None else acc + d
            write_out(sb, acc)

    @pl.when(bounded_ref[0] == 0)
    def _():
        n_loc = tk // LANES
        n_all = (tk + lc) // LANES
        groups = [slice(r * SOFTMAX_ROWS, (r + 1) * SOFTMAX_ROWS) for r in range(2 * tq // SOFTMAX_ROWS)]
        for sb in range(Q_ROWS // SUB_ROWS):
            ks, variant, q_stack = sub_block(sb)

            def logits(rws, c, minus=None):
                cols = slice(c * LANES, (c + 1) * LANES)
                t = s_ref[rws, cols]
                if minus is not None:
                    t = t - minus
                return t + bias_ref[variant, rws, cols] if c < n_loc else t

            kall_ref[0:tk, :] = k_ref[pl.ds(ks, tk), :]
            vall_ref[0:tk, 0:LANES] = v_ref[pl.ds(ks, tk), :]
            s_ref[...] = lax.dot_general(q_stack, kall_ref[...], nt, preferred_element_type=f32)
            for rws in groups:
                acc = logits(rws, 0)
                for c in range(1, n_all):
                    acc = jnp.maximum(acc, logits(rws, c))
                m_ref[rws, :] = jnp.broadcast_to(jnp.max(acc, axis=-1, keepdims=True), (SOFTMAX_ROWS, LANES))
            for rws in groups:
                m = m_ref[rws, :]
                for c in range(n_all):
                    p_ref[rws, c * LANES:(c + 1) * LANES] = jnp.exp(logits(rws, c, minus=m)).astype(bf16)
            write_out(sb, jnp.dot(p_ref[...], vall_ref[...], preferred_element_type=f32))


def _attn_call(bounded, qkv, kvc, rpb_lanes, n_heads, rows):
    s = qkv.shape[0]
    d_attn = n_heads * HEAD_DIM
    slabs = d_attn // LANES
    tq = SUB_ROWS * GRID_W
    tk = K_ROWS * GRID_W
    nblk = rows // Q_ROWS
    assert rows % Q_ROWS == 0 and Q_ROWS % SUB_ROWS == 0 and rows >= K_ROWS + SUB_ROWS
    assert SUB_ROWS % (WIN_H // 2) == 0
    lc = kvc.shape[0]
    assert tk % (2 * LANES) == 0 and lc % (2 * LANES) == 0
    grid_spec = pltpu.PrefetchScalarGridSpec(
        num_scalar_prefetch=1,
        grid=(slabs, nblk),
        in_specs=[
            pl.BlockSpec((Q_ROWS * GRID_W, LANES), lambda h, i, b: (i, h)),
            pl.BlockSpec((s, LANES), lambda h, i, b: (0, slabs + h)),
            pl.BlockSpec((s, LANES), lambda h, i, b: (0, 2 * slabs + h)),
            pl.BlockSpec((lc, LANES), lambda h, i, b: (0, h)),
            pl.BlockSpec((lc, LANES), lambda h, i, b: (0, slabs + h)),
            pl.BlockSpec((HEADS_PER_SLAB, N_TILES, LANES), lambda h, i, b: (h, 0, 0)),
        ],
        out_specs=pl.BlockSpec((Q_ROWS * GRID_W, LANES), lambda h, i, b: (i, h)),
        scratch_shapes=[
            pltpu.VMEM((HEADS_PER_SLAB, N_TILES, GRID_W, LANES), f32),
            pltpu.VMEM((3, HEADS_PER_SLAB * tq, tk), f32),
            pltpu.VMEM((tk + lc, LANES), bf16),
            pltpu.VMEM((tk + lc, 2 * LANES), bf16),
            pltpu.VMEM((HEADS_PER_SLAB * tq, tk + lc), f32),
            pltpu.VMEM((HEADS_PER_SLAB * tq, tk + lc), bf16),
            pltpu.VMEM((HEADS_PER_SLAB * tq, LANES), f32),
        ],
    )
    return pl.pallas_call(
        functools.partial(_attn_kernel, rows),
        grid_spec=grid_spec,
        out_shape=jax.ShapeDtypeStruct((s, d_attn), bf16),
        compiler_params=_params("arbitrary", "arbitrary"),
        name="attn",
    )(bounded, qkv, qkv, qkv, kvc, kvc, rpb_lanes)


def _rms(x, g):
    ms = jnp.mean(x * x, axis=-1, keepdims=True)
    return x * lax.rsqrt(ms + EPS) * g


def _out_kernel(attn_ref, bg_ref, z_ref, zp_ref, zn_ref, x_ref, cw_ref, cb_ref, ga_ref, gc_ref, gate_ref,
                w_ref, o_ref):
    i = pl.program_id(0)
    nt = pl.num_programs(0)
    tm, dc = z_ref.shape
    da = attn_ref.shape[1]
    z = z_ref[...].astype(f32)
    halo = zp_ref.shape[0]
    before = jnp.where(i > 0, zp_ref[halo - 1:halo, :].astype(f32), 0.0)
    after = jnp.where(i < nt - 1, zn_ref[0:1, :].astype(f32), 0.0)
    row = lax.broadcasted_iota(jnp.int32, (tm, 1), 0)
    z_m = jnp.where(row == 0, before, pltpu.roll(z, 1, 0))
    z_p = jnp.where(row == tm - 1, after, pltpu.roll(z, tm - 1, 0))
    y = cb_ref[...] + z_m * cw_ref[0:1, :]
    y = y + z * cw_ref[1:2, :]
    y = y + z_p * cw_ref[2:3, :]
    conv = bg_ref[...].astype(f32) * y
    cn = _rms(conv, gc_ref[...]).astype(bf16)
    an = _rms(attn_ref[...].astype(f32), ga_ref[...]).astype(bf16)
    proj = (jnp.dot(an, w_ref[0:da, :], preferred_element_type=f32)
            + jnp.dot(cn, w_ref[da:da + dc, :], preferred_element_type=f32))
    o_ref[...] = x_ref[...] + gate_ref[...] * proj


def _out_call(attn, mixed, x, conv_w, conv_b, ga, gc, gate, w_out, tm, bg_col, z_col, halo=16):
    s, d = x.shape
    da = attn.shape[1]
    dc = conv_w.shape[1]
    per = tm // halo
    nh = s // halo
    return pl.pallas_call(
        _out_kernel,
        grid=(s // tm,),
        in_specs=[
            pl.BlockSpec((tm, da), lambda i: (i, 0)),
            pl.BlockSpec((tm, dc), lambda i: (i, bg_col)),
            pl.BlockSpec((tm, dc), lambda i: (i, z_col)),
            pl.BlockSpec((halo, dc), lambda i: (jnp.maximum(i * per - 1, 0), z_col)),
            pl.BlockSpec((halo, dc), lambda i: (jnp.minimum((i + 1) * per, nh - 1), z_col)),
            pl.BlockSpec((tm, d), lambda i: (i, 0)),
            pl.BlockSpec((3, dc), lambda i: (0, 0)),
            pl.BlockSpec((1, dc), lambda i: (0, 0)),
            pl.BlockSpec((1, da), lambda i: (0, 0)),
            pl.BlockSpec((1, dc), lambda i: (0, 0)),
            pl.BlockSpec((1, d), lambda i: (0, 0)),
            pl.BlockSpec((da + dc, d), lambda i: (0, 0)),
        ],
        out_specs=pl.BlockSpec((tm, d), lambda i: (i, 0)),
        out_shape=jax.ShapeDtypeStruct((s, d), f32),
        compiler_params=_params("arbitrary"),
        name="out",
    )(attn, mixed, mixed, mixed, mixed, x, conv_w, conv_b.reshape(1, dc), ga.reshape(1, da),
      gc.reshape(1, dc), gate.reshape(1, d), w_out)


def _rope_tables(rows):
    nf = HEAD_DIM // 4
    inv = ROPE_BASE ** (-jnp.arange(nf, dtype=f32) / nf)
    zero = jnp.zeros((1, nf), f32)

    def lanes(pos, first_half):
        ang = jnp.arange(pos, dtype=jnp.int32).astype(f32)[:, None] * inv[None, :]
        cos, sin = jnp.cos(ang), jnp.sin(ang)
        z = jnp.broadcast_to(zero, cos.shape)
        cos_head = [cos, cos, z, z] if first_half else [z, z, cos, cos]
        sin_head = [-sin, sin, z, z] if first_half else [z, z, -sin, sin]
        table = lambda parts: jnp.tile(jnp.concatenate(parts, axis=-1), (1, HEADS_PER_SLAB))
        return jnp.stack([table(cos_head), table(sin_head)])

    return lanes(rows, True), lanes(GRID_W, False)


def _logits_bounded(q_gain, k_gain, rpb):
    bound = (HEAD_DIM ** 0.5) * 1.02 * jnp.max(jnp.abs(q_gain)) * jnp.max(jnp.abs(k_gain)) + jnp.max(jnp.abs(rpb))
    return (bound <= LOGIT_BOUND).astype(jnp.int32).reshape(1)


def _rpb_lanes(rpb):
    h, nr, ncol = rpb.shape
    half = jnp.pad(rpb, ((0, 0), (0, N_TILES - nr), (0, GRID_W - ncol)))
    return jnp.concatenate([half, half], axis=-1)


def kernel(x, c, ctx, c_ctx, w_ada, b_ada, ff1_norm, ff1_w_in, ff1_w_out, mix_norm, w_in, q_norm, k_norm,
           rpb, conv_w, conv_b, out_norm_attn, out_norm_conv, w_out, ff2_norm, ff2_w_in, ff2_w_out):
    batch, s, d = x.shape
    depth = w_ada.shape[0]
    assert batch == 1 and depth == 1
    n_heads = rpb.shape[1]
    d_attn = n_heads * HEAD_DIM
    d_conv = conv_w.shape[-1]
    assert d_attn == 1024 and d_conv == 1024
    rows = s // GRID_W

    xs = x[0]
    cs = ctx[0]
    l = 0

    mods = _ada_call(jnp.stack([c[0], c_ctx], axis=1), w_ada[l], b_ada[l])
    mx = mods[0].reshape(N_MOD, d)
    mc = mods[1].reshape(N_MOD, d)

    w1_in = ff1_w_in[l].astype(bf16)
    w1_out = ff1_w_out[l].astype(bf16)
    w2_in = ff2_w_in[l].astype(bf16)
    w2_out = ff2_w_out[l].astype(bf16)
    w_mix = w_in[l].astype(bf16)
    w_o = w_out[l].astype(bf16)

    rope = _rope_tables(rows)

    xs = _ffn_call(xs, mx[0:3], ff1_norm[l], w1_in, w1_out, tm=1024, tf=512)
    cs = _ffn_call(cs, mc[0:3], ff1_norm[l], w1_in, w1_out, tm=cs.shape[0], tf=512)

    mixed = _proj_call(xs, mx[3:5], mix_norm[l], w_mix, q_norm[l], k_norm[l], rope,
                       kinds=("q", "k", "plain", "plain", "mul_lo", "mul_hi"),
                       w_blocks=((0, 1), (2, 3), (4, 5), (6, 7), (8, 10), (9, 11)),
                       out_cols=(0, 1, 2, 3, 4, 4), tm=1024)
    kvc = _proj_call(cs, mc[3:5], mix_norm[l], w_mix, q_norm[l], k_norm[l], None,
                     kinds=("k_nopos", "plain"), w_blocks=((2, 3), (4, 5)), out_cols=(0, 1),
                     tm=cs.shape[0])

    attn = _attn_call(_logits_bounded(q_norm[l], k_norm[l], rpb[l]), mixed, kvc, _rpb_lanes(rpb[l]), n_heads, rows)
    xs = _out_call(attn, mixed, xs, conv_w[l], conv_b[l], out_norm_attn[l], out_norm_conv[l], mx[5], w_o,
                   tm=512, bg_col=3, z_col=4)

    xs = _ffn_call(xs, mx[6:9], ff2_norm[l], w2_in, w2_out, tm=1024, tf=512)
    return xs[None]
```

```python
import functools

import numpy as np
import jax
import jax.numpy as jnp
from jax import lax
from jax.experimental import pallas as pl
from jax.experimental.pallas import tpu as pltpu

GRID_W = 64
HEAD_DIM = 64
WIN_H = 8
WIN_W = 16
ROPE_BASE = 10000.0
EPS = 1e-6
N_MOD = 9

LANES = 128
HEADS_PER_SLAB = LANES // HEAD_DIM
VMEM_LIMIT = 60 * 1024 * 1024
NEG = -1e30

Q_ROWS = 32
SUB_ROWS = 4
K_ROWS = SUB_ROWS + WIN_H
N_TILES = 2 * WIN_H
NO_TILE = N_TILES - 1
SOFTMAX_ROWS = 16
LOGIT_BOUND = 20.0

f32 = jnp.float32
bf16 = jnp.bfloat16


def _params(*sem, flags=None):
    return pltpu.CompilerParams(dimension_semantics=sem, vmem_limit_bytes=VMEM_LIMIT, flags=flags)


def _pick(n, values):
    out = jnp.int32(values[-1])
    for idx in range(len(values) - 2, -1, -1):
        out = jnp.where(n == idx, values[idx], out)
    return out


def _ada_kernel(c_ref, w_ref, b_ref, o_ref):
    c = c_ref[...]
    s = c * jax.nn.sigmoid(c)
    w = w_ref[...]
    r0 = jnp.sum(w * s[:, 0:1], axis=0, keepdims=True)
    r1 = jnp.sum(w * s[:, 1:2], axis=0, keepdims=True)
    o_ref[...] = jnp.concatenate([r0, r1], axis=0) + b_ref[...]


def _ada_call(cvecs_t, w_ada, b_ada, tn=2048):
    d, n = w_ada.shape
    return pl.pallas_call(
        _ada_kernel,
        grid=(n // tn,),
        in_specs=[
            pl.BlockSpec((d, 2), lambda j: (0, 0)),
            pl.BlockSpec((d, tn), lambda j: (0, j)),
            pl.BlockSpec((1, tn), lambda j: (0, j)),
        ],
        out_specs=pl.BlockSpec((2, tn), lambda j: (0, j)),
        out_shape=jax.ShapeDtypeStruct((2, n), f32),
        compiler_params=_params("arbitrary"),
        name="ada",
    )(cvecs_t, w_ada, b_ada.reshape(1, n))


def _norm_mod_into(x_ref, g_ref, mod_ref, h_ref, inv_ref, copy_ref=None, rows_per_chunk=256):
    tm, d = x_ref.shape
    rc = min(rows_per_chunk, tm)

    def inv_rms(r, carry):
        rows = pl.ds(pl.multiple_of(r * rc, rc), rc)
        x = x_ref[rows, :]
        ms = jnp.sum(x * x, axis=-1, keepdims=True) * (1.0 / d)
        inv_ref[rows, :] = jnp.broadcast_to(lax.rsqrt(ms + EPS), (rc, LANES))
        return carry

    def scale(r, carry):
        rows = pl.ds(pl.multiple_of(r * rc, rc), rc)
        inv = inv_ref[rows, :]
        for c in range(d // LANES):
            cols = slice(c * LANES, (c + 1) * LANES)
            x = x_ref[rows, cols]
            y = x * inv * g_ref[:, cols]
            h_ref[rows, cols] = (y * (1.0 + mod_ref[1:2, cols]) + mod_ref[0:1, cols]).astype(bf16)
            if copy_ref is not None:
                copy_ref[rows, cols] = x
        return carry

    lax.fori_loop(0, tm // rc, inv_rms, 0)
    lax.fori_loop(0, tm // rc, scale, 0)


def _ffn_kernel(n_split, x_ref, mod_ref, g_ref, wa_ref, wb_ref, wo_ref, o_ref, h_ref, inv_ref):
    @pl.when(pl.program_id(1) == 0)
    def _():
        _norm_mod_into(x_ref, g_ref, mod_ref, h_ref, inv_ref, copy_ref=o_ref)

    h = h_ref[...]
    a = jnp.dot(h, wa_ref[...], preferred_element_type=f32)
    b = jnp.dot(h, wb_ref[...], preferred_element_type=f32)
    act = (a * jax.nn.sigmoid(a) * b).astype(bf16)
    half_gate = 0.5 * mod_ref[2:3, :]
    w = o_ref.shape[1] // n_split
    for c in range(n_split):
        cols = slice(c * w, (c + 1) * w)
        o_ref[:, cols] += half_gate[:, cols] * jnp.dot(act, wo_ref[:, cols], preferred_element_type=f32)


def _ffn_call(x, mods, g, w_in, w_out, tm, tf, n_split=4):
    m, d = x.shape
    ff = w_out.shape[0]
    nf = ff // tf
    return pl.pallas_call(
        functools.partial(_ffn_kernel, n_split),
        grid=(m // tm, nf),
        in_specs=[
            pl.BlockSpec((tm, d), lambda i, j: (i, 0)),
            pl.BlockSpec((3, d), lambda i, j: (0, 0)),
            pl.BlockSpec((1, d), lambda i, j: (0, 0)),
            pl.BlockSpec((d, tf), lambda i, j: (0, j)),
            pl.BlockSpec((d, tf), lambda i, j: (0, j + nf)),
            pl.BlockSpec((tf, d), lambda i, j: (j, 0)),
        ],
        out_specs=pl.BlockSpec((tm, d), lambda i, j: (i, 0)),
        out_shape=jax.ShapeDtypeStruct((m, d), f32),
        scratch_shapes=[pltpu.VMEM((tm, d), bf16), pltpu.VMEM((tm, LANES), f32)],
        compiler_params=_params("arbitrary", "arbitrary"),
        name="ffn",
    )(x, mods, g.reshape(1, d), w_in, w_in, w_out)


def _proj_kernel(kinds, *refs):
    use_rope = any(k in ("q", "k") for k in kinds)
    if use_rope:
        (x_ref, mod_ref, g_ref, wa_ref, wb_ref, qg_ref, kg_ref, rrow_ref, rcol_ref,
         o_ref, h_ref, inv_ref, cos_ref, sin_ref) = refs
    else:
        x_ref, mod_ref, g_ref, wa_ref, wb_ref, qg_ref, kg_ref, o_ref, h_ref, inv_ref = refs
    i = pl.program_id(0)
    n = pl.program_id(1)
    tm = o_ref.shape[0]
    tw = wa_ref.shape[1]

    @pl.when(n == 0)
    def _():
        _norm_mod_into(x_ref, g_ref, mod_ref, h_ref, inv_ref)
        if use_rope:
            for r in range(tm // GRID_W):
                tok = slice(r * GRID_W, (r + 1) * GRID_W)
                grid_row = i * (tm // GRID_W) + r
                cos_ref[tok, :] = rrow_ref[0, pl.ds(grid_row, 1), :] + rcol_ref[0]
                sin_ref[tok, :] = rrow_ref[1, pl.ds(grid_row, 1), :] + rcol_ref[1]

    pair = 2 * LANES
    row = lax.broadcasted_iota(jnp.int32, (pair, pair), 0) // HEAD_DIM
    col = lax.broadcasted_iota(jnp.int32, (pair, pair), 1) // HEAD_DIM
    same_head = (row == col).astype(bf16)
    lane = lax.broadcasted_iota(jnp.int32, (1, LANES), 1)
    quarter = HEAD_DIM // 4
    upper = (lane & quarter) != 0

    def product(w_ref):
        return jnp.dot(h_ref[...], w_ref[...], preferred_element_type=f32)

    def headwise(gain_ref, rope):
        gain = gain_ref[...]
        for part, w_ref in enumerate((wa_ref, wb_ref)):
            res = product(w_ref)
            for s in range(tw // pair):
                y = res[:, s * pair:(s + 1) * pair]
                ssq = jnp.dot((y * y).astype(bf16), same_head, preferred_element_type=f32)
                y = y * lax.rsqrt(ssq * (1.0 / HEAD_DIM) + EPS) * gain
                for u in range(2):
                    yu = y[:, u * LANES:(u + 1) * LANES]
                    if rope:
                        partner = jnp.where(upper, pltpu.roll(yu, quarter, 1), pltpu.roll(yu, LANES - quarter, 1))
                        yu = yu * cos_ref[...] + partner * sin_ref[...]
                    lo = part * tw + s * pair + u * LANES
                    o_ref[:, lo:lo + LANES] = yu.astype(o_ref.dtype)

    def plain():
        o_ref[:, :tw] = product(wa_ref).astype(o_ref.dtype)
        o_ref[:, tw:] = product(wb_ref).astype(o_ref.dtype)

    def multiply(lo):
        o_ref[:, lo:lo + tw] = (product(wa_ref) * product(wb_ref)).astype(o_ref.dtype)

    for idx, kind in enumerate(kinds):
        @pl.when(n == idx)
        def _(kind=kind):
            if kind == "q":
                headwise(qg_ref, True)
            elif kind == "k":
                headwise(kg_ref, True)
            elif kind == "k_nopos":
                headwise(kg_ref, False)
            elif kind == "plain":
                plain()
            elif kind == "mul_lo":
                multiply(0)
            elif kind == "mul_hi":
                multiply(tw)
            else:
                raise ValueError(kind)


def _proj_call(x, mods, g, w, qg, kg, rope, kinds, w_blocks, out_cols, tm, tn=1024):
    m, d = x.shape
    tw = tn // 2
    use_rope = rope is not None
    n_out = max(out_cols) + 1
    blk_a = tuple(b[0] for b in w_blocks)
    blk_b = tuple(b[1] for b in w_blocks)
    out_cols = tuple(out_cols)
    pair_gain = lambda v: jnp.tile(v, 2 * HEADS_PER_SLAB)[None, :]
    assert tm % GRID_W == 0

    in_specs = [
        pl.BlockSpec((tm, d), lambda i, n: (i, 0)),
        pl.BlockSpec((2, d), lambda i, n: (0, 0)),
        pl.BlockSpec((1, d), lambda i, n: (0, 0)),
        pl.BlockSpec((d, tw), lambda i, n: (0, _pick(n, blk_a))),
        pl.BlockSpec((d, tw), lambda i, n: (0, _pick(n, blk_b))),
        pl.BlockSpec((1, 2 * LANES), lambda i, n: (0, 0)),
        pl.BlockSpec((1, 2 * LANES), lambda i, n: (0, 0)),
    ]
    args = [x, mods, g.reshape(1, d), w, w, pair_gain(qg), pair_gain(kg)]
    scratch = [pltpu.VMEM((tm, d), bf16), pltpu.VMEM((tm, LANES), f32)]
    if use_rope:
        in_specs += [pl.BlockSpec(t.shape, lambda i, n: (0, 0, 0)) for t in rope]
        args += list(rope)
        scratch += [pltpu.VMEM((tm, LANES), f32)] * 2
    return pl.pallas_call(
        functools.partial(_proj_kernel, tuple(kinds)),
        grid=(m // tm, len(kinds)),
        in_specs=in_specs,
        out_specs=pl.BlockSpec((tm, tn), lambda i, n: (i, _pick(n, out_cols))),
        out_shape=jax.ShapeDtypeStruct((m, n_out * tn), bf16),
        scratch_shapes=scratch,
        compiler_params=_params("arbitrary", "arbitrary"),
        name="proj",
    )(*args)


def _row_window_tables(rows):
    tables = []
    for r0 in (0, SUB_ROWS, rows - SUB_ROWS):
        ks = min(max(r0 - WIN_H // 2, 0), rows - K_ROWS)
        tab = [[NO_TILE] * K_ROWS for _ in range(SUB_ROWS)]
        for a in range(SUB_ROWS):
            r = r0 + a
            rs = min(max(r - WIN_H // 2, 0), rows - WIN_H)
            assert ks <= rs and rs + WIN_H <= ks + K_ROWS
            for b in range(K_ROWS):
                kr = ks + b
                if rs <= kr < rs + WIN_H:
                    tab[a][b] = kr - r + (WIN_H - 1)
        tables.append(tab)
    return tables


def _attn_kernel(rows, bounded_ref, q_ref, k_ref, v_ref, kc_ref, vc_ref, rpb_ref, o_ref,
                 tile_ref, bias_ref, kall_ref, vall_ref, s_ref, p_ref, m_ref):
    i = pl.program_id(1)
    nblk = pl.num_programs(1)
    tq = SUB_ROWS * GRID_W
    tk = K_ROWS * GRID_W
    lc = kc_ref.shape[0]
    key_tile = 2 * LANES
    lane = lax.broadcasted_iota(jnp.int32, (1, LANES), 1)
    first_head = lane < HEAD_DIM
    ones_tile = jnp.ones((key_tile, LANES), bf16)
    nt = (((1,), (1,)), ((), ()))

    def build_tiles():
        qc = lax.broadcasted_iota(jnp.int32, (GRID_W, LANES), 0)
        kc = lax.broadcasted_iota(jnp.int32, (GRID_W, LANES), 1) % GRID_W
        cs = jnp.clip(qc - WIN_W // 2, 0, GRID_W - WIN_W)
        in_window = (kc >= cs) & (kc < cs + WIN_W)
        for hh in range(HEADS_PER_SLAB):
            for e in range(NO_TILE):
                base = jnp.broadcast_to(rpb_ref[hh, e:e + 1, :], (GRID_W, LANES))
                shifted = pltpu.roll(base, LANES - (WIN_W - 1), 1, stride=1, stride_axis=0)
                tile_ref[hh, e] = jnp.where(in_window, shifted, NEG)
            tile_ref[hh, NO_TILE] = jnp.full((GRID_W, LANES), NEG, f32)

    def assemble(variant, tab):
        for hh in range(HEADS_PER_SLAB):
            for a in range(SUB_ROWS):
                for bp in range(K_ROWS // 2):
                    e0, e1 = tab[a][2 * bp], tab[a][2 * bp + 1]
                    if e0 == e1:
                        tile = tile_ref[hh, e0]
                    else:
                        tile = jnp.where(first_head, tile_ref[hh, e0], tile_ref[hh, e1])
                    lo = hh * tq + a * GRID_W
                    bias_ref[variant, lo:lo + GRID_W, bp * LANES:(bp + 1) * LANES] = tile

    @pl.when(i == 0)
    def _():
        build_tiles()
        for variant, tab in enumerate(_row_window_tables(rows)):
            assemble(variant, tab)
        kall_ref[tk:, :] = kc_ref[...]
        vall_ref[tk:, 0:LANES] = vc_ref[...]
        vall_ref[:, LANES:] = jnp.ones((tk + lc, LANES), bf16)

    def sub_block(sb):
        r0 = i * Q_ROWS + sb * SUB_ROWS
        ks = jnp.clip(r0 - WIN_H // 2, 0, rows - K_ROWS) * GRID_W
        ks = pl.multiple_of(ks, SUB_ROWS * GRID_W)
        if sb == 0:
            variant = jnp.where(i == 0, 0, 1)
        elif sb == Q_ROWS // SUB_ROWS - 1:
            variant = jnp.where(i == nblk - 1, 2, 1)
        else:
            variant = 1
        q = q_ref[sb * tq:(sb + 1) * tq, :] * jnp.asarray(HEAD_DIM ** -0.5, bf16)
        zero = jnp.zeros_like(q)
        q_stack = jnp.concatenate([jnp.where(first_head, q, zero), jnp.where(first_head, zero, q)], axis=0)
        return ks, variant, q_stack

    def write_out(sb, prod):
        o = prod[:, :LANES] * (1.0 / prod[:, LANES:])
        o_ref[sb * tq:(sb + 1) * tq, :] = jnp.where(first_head, o[:tq], o[tq:]).astype(o_ref.dtype)

    @pl.when(bounded_ref[0] != 0)
    def _():
        for sb in range(Q_ROWS // SUB_ROWS):
            ks, variant, q_stack = sub_block(sb)
            acc = None
            for lo in range(0, tk + lc, key_tile):
                if lo < tk:
                    keys = pl.ds(ks + lo, key_tile)
                    k_tile, v_tile = k_ref[keys, :], v_ref[keys, :]
                else:
                    k_tile, v_tile = kc_ref[lo - tk:lo - tk + key_tile, :], vc_ref[lo - tk:lo - tk + key_tile, :]
                t = lax.dot_general(q_stack, k_tile, nt, preferred_element_type=f32)
                if lo < tk:
                    t = t + bias_ref[variant, :, lo:lo + key_tile]
                d = jnp.dot(jnp.exp(t).astype(bf16), jnp.concatenate([v_tile, ones_tile], axis=1),
                            preferred_element_type=f32)
                acc = d if acc is None else acc + d
            write_out(sb, acc)

    @pl.when(bounded_ref[0] == 0)
    def _():
        n_loc = tk // LANES
        n_all = (tk + lc) // LANES
        groups = [slice(r * SOFTMAX_ROWS, (r + 1) * SOFTMAX_ROWS) for r in range(2 * tq // SOFTMAX_ROWS)]
        for sb in range(Q_ROWS // SUB_ROWS):
            ks, variant, q_stack = sub_block(sb)

            def logits(rws, c, minus=None):
                cols = slice(c * LANES, (c + 1) * LANES)
                t = s_ref[rws, cols]
                if minus is not None:
                    t = t - minus
                return t + bias_ref[variant, rws, cols] if c < n_loc else t

            kall_ref[0:tk, :] = k_ref[pl.ds(ks, tk), :]
            vall_ref[0:tk, 0:LANES] = v_ref[pl.ds(ks, tk), :]
            s_ref[...] = lax.dot_general(q_stack, kall_ref[...], nt, preferred_element_type=f32)
            for rws in groups:
                acc = logits(rws, 0)
                for c in range(1, n_all):
                    acc = jnp.maximum(acc, logits(rws, c))
                m_ref[rws, :] = jnp.broadcast_to(jnp.max(acc, axis=-1, keepdims=True), (SOFTMAX_ROWS, LANES))
            for rws in groups:
                m = m_ref[rws, :]
                for c in range(n_all):
                    p_ref[rws, c * LANES:(c + 1) * LANES] = jnp.exp(logits(rws, c, minus=m)).astype(bf16)
            write_out(sb, jnp.dot(p_ref[...], vall_ref[...], preferred_element_type=f32))


def _attn_call(bounded, qkv, kvc, rpb_lanes, n_heads, rows):
    s = qkv.shape[0]
    d_attn = n_heads * HEAD_DIM
    slabs = d_attn // LANES
    tq = SUB_ROWS * GRID_W
    tk = K_ROWS * GRID_W
    nblk = rows // Q_ROWS
    assert rows % Q_ROWS == 0 and Q_ROWS % SUB_ROWS == 0 and rows >= K_ROWS + SUB_ROWS
    assert SUB_ROWS % (WIN_H // 2) == 0
    lc = kvc.shape[0]
    assert tk % (2 * LANES) == 0 and lc % (2 * LANES) == 0
    grid_spec = pltpu.PrefetchScalarGridSpec(
        num_scalar_prefetch=1,
        grid=(slabs, nblk),
        in_specs=[
            pl.BlockSpec((Q_ROWS * GRID_W, LANES), lambda h, i, b: (i, h)),
            pl.BlockSpec((s, LANES), lambda h, i, b: (0, slabs + h)),
            pl.BlockSpec((s, LANES), lambda h, i, b: (0, 2 * slabs + h)),
            pl.BlockSpec((lc, LANES), lambda h, i, b: (0, h)),
            pl.BlockSpec((lc, LANES), lambda h, i, b: (0, slabs + h)),
            pl.BlockSpec((HEADS_PER_SLAB, N_TILES, LANES), lambda h, i, b: (h, 0, 0)),
        ],
        out_specs=pl.BlockSpec((Q_ROWS * GRID_W, LANES), lambda h, i, b: (i, h)),
        scratch_shapes=[
            pltpu.VMEM((HEADS_PER_SLAB, N_TILES, GRID_W, LANES), f32),
            pltpu.VMEM((3, HEADS_PER_SLAB * tq, tk), f32),
            pltpu.VMEM((tk + lc, LANES), bf16),
            pltpu.VMEM((tk + lc, 2 * LANES), bf16),
            pltpu.VMEM((HEADS_PER_SLAB * tq, tk + lc), f32),
            pltpu.VMEM((HEADS_PER_SLAB * tq, tk + lc), bf16),
            pltpu.VMEM((HEADS_PER_SLAB * tq, LANES), f32),
        ],
    )
    return pl.pallas_call(
        functools.partial(_attn_kernel, rows),
        grid_spec=grid_spec,
        out_shape=jax.ShapeDtypeStruct((s, d_attn), bf16),
        compiler_params=_params("arbitrary", "arbitrary"),
        name="attn",
    )(bounded, qkv, qkv, qkv, kvc, kvc, rpb_lanes)


def _rms(x, g):
    ms = jnp.mean(x * x, axis=-1, keepdims=True)
    return x * lax.rsqrt(ms + EPS) * g


def _out_kernel(attn_ref, bg_ref, z_ref, zp_ref, zn_ref, x_ref, cw_ref, cb_ref, ga_ref, gc_ref, gate_ref,
                w_ref, o_ref):
    i = pl.program_id(0)
    nt = pl.num_programs(0)
    tm, dc = z_ref.shape
    da = attn_ref.shape[1]
    z = z_ref[...].astype(f32)
    halo = zp_ref.shape[0]
    before = jnp.where(i > 0, zp_ref[halo - 1:halo, :].astype(f32), 0.0)
    after = jnp.where(i < nt - 1, zn_ref[0:1, :].astype(f32), 0.0)
    row = lax.broadcasted_iota(jnp.int32, (tm, 1), 0)
    z_m = jnp.where(row == 0, before, pltpu.roll(z, 1, 0))
    z_p = jnp.where(row == tm - 1, after, pltpu.roll(z, tm - 1, 0))
    y = cb_ref[...] + z_m * cw_ref[0:1, :]
    y = y + z * cw_ref[1:2, :]
    y = y + z_p * cw_ref[2:3, :]
    conv = bg_ref[...].astype(f32) * y
    cn = _rms(conv, gc_ref[...]).astype(bf16)
    an = _rms(attn_ref[...].astype(f32), ga_ref[...]).astype(bf16)
    proj = (jnp.dot(an, w_ref[0:da, :], preferred_element_type=f32)
            + jnp.dot(cn, w_ref[da:da + dc, :], preferred_element_type=f32))
    o_ref[...] = x_ref[...] + gate_ref[...] * proj


def _out_call(attn, mixed, x, conv_w, conv_b, ga, gc, gate, w_out, tm, bg_col, z_col, halo=16):
    s, d = x.shape
    da = attn.shape[1]
    dc = conv_w.shape[1]
    per = tm // halo
    nh = s // halo
    return pl.pallas_call(
        _out_kernel,
        grid=(s // tm,),
        in_specs=[
            pl.BlockSpec((tm, da), lambda i: (i, 0)),
            pl.BlockSpec((tm, dc), lambda i: (i, bg_col)),
            pl.BlockSpec((tm, dc), lambda i: (i, z_col)),
            pl.BlockSpec((halo, dc), lambda i: (jnp.maximum(i * per - 1, 0), z_col)),
            pl.BlockSpec((halo, dc), lambda i: (jnp.minimum((i + 1) * per, nh - 1), z_col)),
            pl.BlockSpec((tm, d), lambda i: (i, 0)),
            pl.BlockSpec((3, dc), lambda i: (0, 0)),
            pl.BlockSpec((1, dc), lambda i: (0, 0)),
            pl.BlockSpec((1, da), lambda i: (0, 0)),
            pl.BlockSpec((1, dc), lambda i: (0, 0)),
            pl.BlockSpec((1, d), lambda i: (0, 0)),
            pl.BlockSpec((da + dc, d), lambda i: (0, 0)),
        ],
        out_specs=pl.BlockSpec((tm, d), lambda i: (i, 0)),
        out_shape=jax.ShapeDtypeStruct((s, d), f32),
        compiler_params=_params("arbitrary"),
        name="out",
    )(attn, mixed, mixed, mixed, mixed, x, conv_w, conv_b.reshape(1, dc), ga.reshape(1, da),
      gc.reshape(1, dc), gate.reshape(1, d), w_out)


def _rope_tables(rows):
    nf = HEAD_DIM // 4
    inv = ROPE_BASE ** (-jnp.arange(nf, dtype=f32) / nf)
    zero = jnp.zeros((1, nf), f32)

    def lanes(pos, first_half):
        ang = jnp.arange(pos, dtype=jnp.int32).astype(f32)[:, None] * inv[None, :]
        cos, sin = jnp.cos(ang), jnp.sin(ang)
        z = jnp.broadcast_to(zero, cos.shape)
        cos_head = [cos, cos, z, z] if first_half else [z, z, cos, cos]
        sin_head = [-sin, sin, z, z] if first_half else [z, z, -sin, sin]
        table = lambda parts: jnp.tile(jnp.concatenate(parts, axis=-1), (1, HEADS_PER_SLAB))
        return jnp.stack([table(cos_head), table(sin_head)])

    return lanes(rows, True), lanes(GRID_W, False)


def _logits_bounded(q_gain, k_gain, rpb):
    bound = (HEAD_DIM ** 0.5) * 1.02 * jnp.max(jnp.abs(q_gain)) * jnp.max(jnp.abs(k_gain)) + jnp.max(jnp.abs(rpb))
    return (bound <= LOGIT_BOUND).astype(jnp.int32).reshape(1)


def _rpb_lanes(rpb):
    h, nr, ncol = rpb.shape
    half = jnp.pad(rpb, ((0, 0), (0, N_TILES - nr), (0, GRID_W - ncol)))
    return jnp.concatenate([half, half], axis=-1)


def kernel(x, c, ctx, c_ctx, w_ada, b_ada, ff1_norm, ff1_w_in, ff1_w_out, mix_norm, w_in, q_norm, k_norm,
           rpb, conv_w, conv_b, out_norm_attn, out_norm_conv, w_out, ff2_norm, ff2_w_in, ff2_w_out):
    batch, s, d = x.shape
    depth = w_ada.shape[0]
    assert batch == 1 and depth == 1
    n_heads = rpb.shape[1]
    d_attn = n_heads * HEAD_DIM
    d_conv = conv_w.shape[-1]
    assert d_attn == 1024 and d_conv == 1024
    rows = s // GRID_W

    xs = x[0]
    cs = ctx[0]
    l = 0

    mods = _ada_call(jnp.stack([c[0], c_ctx], axis=1), w_ada[l], b_ada[l])
    mx = mods[0].reshape(N_MOD, d)
    mc = mods[1].reshape(N_MOD, d)

    w1_in = ff1_w_in[l].astype(bf16)
    w1_out = ff1_w_out[l].astype(bf16)
    w2_in = ff2_w_in[l].astype(bf16)
    w2_out = ff2_w_out[l].astype(bf16)
    w_mix = w_in[l].astype(bf16)
    w_o = w_out[l].astype(bf16)

    rope = _rope_tables(rows)

    xs = _ffn_call(xs, mx[0:3], ff1_norm[l], w1_in, w1_out, tm=1024, tf=512)
    cs = _ffn_call(cs, mc[0:3], ff1_norm[l], w1_in, w1_out, tm=cs.shape[0], tf=512)

    mixed = _proj_call(xs, mx[3:5], mix_norm[l], w_mix, q_norm[l], k_norm[l], rope,
                       kinds=("q", "k", "plain", "plain", "mul_lo", "mul_hi"),
                       w_blocks=((0, 1), (2, 3), (4, 5), (6, 7), (8, 10), (9, 11)),
                       out_cols=(0, 1, 2, 3, 4, 4), tm=1024)
    kvc = _proj_call(cs, mc[3:5], mix_norm[l], w_mix, q_norm[l], k_norm[l], None,
                     kinds=("k_nopos", "plain"), w_blocks=((2, 3), (4, 5)), out_cols=(0, 1),
                     tm=cs.shape[0])

    attn = _attn_call(_logits_bounded(q_norm[l], k_norm[l], rpb[l]), mixed, kvc, _rpb_lanes(rpb[l]), n_heads, rows)
    xs = _out_call(attn, mixed, xs, conv_w[l], conv_b[l], out_norm_attn[l], out_norm_conv[l], mx[5], w_o,
                   tm=512, bg_col=3, z_col=4)

    xs = _ffn_call(xs, mx[6:9], ff2_norm[l], w2_in, w2_out, tm=1024, tf=512)
    return xs[None]
```

```python
import functools

import numpy as np
import jax
import jax.numpy as jnp
from jax import lax
from jax.experimental import pallas as pl
from jax.experimental.pallas import tpu as pltpu

GRID_W = 64
HEAD_DIM = 64
WIN_H = 8
WIN_W = 16
ROPE_BASE = 10000.0
EPS = 1e-6
N_MOD = 9

LANES = 128
HEADS_PER_SLAB = LANES // HEAD_DIM
VMEM_LIMIT = 60 * 1024 * 1024
NEG = -1e30

Q_ROWS = 32
SUB_ROWS = 4
K_ROWS = SUB_ROWS + WIN_H
N_TILES = 2 * WIN_H
NO_TILE = N_TILES - 1
LOGIT_BOUND = 20.0

f32 = jnp.float32
bf16 = jnp.bfloat16


def _params(*sem, flags=None):
    return pltpu.CompilerParams(dimension_semantics=sem, vmem_limit_bytes=VMEM_LIMIT, flags=flags)


def _pick(n, values):
    out = jnp.int32(values[-1])
    for idx in range(len(values) - 2, -1, -1):
        out = jnp.where(n == idx, values[idx], out)
    return out


def _ada_kernel(c_ref, w_ref, b_ref, o_ref):
    c = c_ref[...]
    s = c * jax.nn.sigmoid(c)
    w = w_ref[...]
    r0 = jnp.sum(w * s[:, 0:1], axis=0, keepdims=True)
    r1 = jnp.sum(w * s[:, 1:2], axis=0, keepdims=True)
    o_ref[...] = jnp.concatenate([r0, r1], axis=0) + b_ref[...]


def _ada_call(cvecs_t, w_ada, b_ada, tn=2048):
    d, n = w_ada.shape
    return pl.pallas_call(
        _ada_kernel,
        grid=(n // tn,),
        in_specs=[
            pl.BlockSpec((d, 2), lambda j: (0, 0)),
            pl.BlockSpec((d, tn), lambda j: (0, j)),
            pl.BlockSpec((1, tn), lambda j: (0, j)),
        ],
        out_specs=pl.BlockSpec((2, tn), lambda j: (0, j)),
        out_shape=jax.ShapeDtypeStruct((2, n), f32),
        compiler_params=_params("arbitrary"),
        name="ada",
    )(cvecs_t, w_ada, b_ada.reshape(1, n))


def _norm_mod_into(x_ref, g_ref, mod_ref, h_ref, inv_ref, copy_ref=None, rows_per_chunk=256):
    tm, d = x_ref.shape
    rc = min(rows_per_chunk, tm)

    def inv_rms(r, carry):
        rows = pl.ds(pl.multiple_of(r * rc, rc), rc)
        x = x_ref[rows, :]
        ms = jnp.sum(x * x, axis=-1, keepdims=True) * (1.0 / d)
        inv_ref[rows, :] = jnp.broadcast_to(lax.rsqrt(ms + EPS), (rc, LANES))
        return carry

    def scale(r, carry):
        rows = pl.ds(pl.multiple_of(r * rc, rc), rc)
        inv = inv_ref[rows, :]
        for c in range(d // LANES):
            cols = slice(c * LANES, (c + 1) * LANES)
            x = x_ref[rows, cols]
            gain = g_ref[:, cols] * (1.0 + mod_ref[1:2, cols])
            h_ref[rows, cols] = (x * inv * gain + mod_ref[0:1, cols]).astype(bf16)
            if copy_ref is not None:
                copy_ref[rows, cols] = x
        return carry

    lax.fori_loop(0, tm // rc, inv_rms, 0)
    lax.fori_loop(0, tm // rc, scale, 0)


def _side_cast(src_refs, dst_refs):
    for src, dst in zip(src_refs, dst_refs):
        dst[...] = src[...].astype(dst.dtype)


def _side_cast_specs(side):
    in_specs = [pl.BlockSpec(blk, imap) for _, blk, imap in side]
    out_shapes = [jax.ShapeDtypeStruct(a.shape, bf16) for a, _, _ in side]
    return in_specs, [a for a, _, _ in side], list(in_specs), out_shapes


def _ffn_kernel(n_split, n_side, x_ref, mod_ref, g_ref, wa_ref, wb_ref, wo_ref, *rest):
    side_in, o_ref, side_out = rest[:n_side], rest[n_side], rest[n_side + 1:2 * n_side + 1]
    h_ref, inv_ref = rest[2 * n_side + 1:]

    @pl.when(pl.program_id(1) == 0)
    def _():
        _norm_mod_into(x_ref, g_ref, mod_ref, h_ref, inv_ref, copy_ref=o_ref)

    _side_cast(side_in, side_out)
    h = h_ref[...]
    a = jnp.dot(h, wa_ref[...], preferred_element_type=f32)
    b = jnp.dot(h, wb_ref[...], preferred_element_type=f32)
    act = (a * jax.nn.sigmoid(a) * b).astype(bf16)
    half_gate = 0.5 * mod_ref[2:3, :]
    w = o_ref.shape[1] // n_split
    for c in range(n_split):
        cols = slice(c * w, (c + 1) * w)
        o_ref[:, cols] += half_gate[:, cols] * jnp.dot(act, wo_ref[:, cols], preferred_element_type=f32)


def _ffn_call(x, mods, g, w_in, w_out, tm, tf, side=(), n_split=4):
    m, d = x.shape
    ff = w_out.shape[0]
    nf = ff // tf
    side_in, side_args, side_out, side_shapes = _side_cast_specs(side)
    out = pl.pallas_call(
        functools.partial(_ffn_kernel, n_split, len(side)),
        grid=(m // tm, nf),
        in_specs=[
            pl.BlockSpec((tm, d), lambda i, j: (i, 0)),
            pl.BlockSpec((3, d), lambda i, j: (0, 0)),
            pl.BlockSpec((1, d), lambda i, j: (0, 0)),
            pl.BlockSpec((d, tf), lambda i, j: (0, j)),
            pl.BlockSpec((d, tf), lambda i, j: (0, j + nf)),
            pl.BlockSpec((tf, d), lambda i, j: (j, 0)),
        ] + side_in,
        out_specs=[pl.BlockSpec((tm, d), lambda i, j: (i, 0))] + side_out,
        out_shape=[jax.ShapeDtypeStruct((m, d), f32)] + side_shapes,
        scratch_shapes=[pltpu.VMEM((tm, d), bf16), pltpu.VMEM((tm, LANES), f32)],
        compiler_params=_params("arbitrary", "arbitrary"),
        name="ffn",
    )(x, mods, g.reshape(1, d), w_in, w_in, w_out, *side_args)
    return out if side else out[0]


def _proj_kernel(kinds, *refs):
    use_rope = any(k in ("q", "k") for k in kinds)
    if use_rope:
        (x_ref, mod_ref, g_ref, wa_ref, wb_ref, qg_ref, kg_ref, rrow_ref, rcol_ref,
         o_ref, h_ref, inv_ref, cos_ref, sin_ref) = refs
    else:
        x_ref, mod_ref, g_ref, wa_ref, wb_ref, qg_ref, kg_ref, o_ref, h_ref, inv_ref = refs
    i = pl.program_id(0)
    n = pl.program_id(1)
    tm = o_ref.shape[0]
    tw = wa_ref.shape[1]

    @pl.when(n == 0)
    def _():
        _norm_mod_into(x_ref, g_ref, mod_ref, h_ref, inv_ref)
        if use_rope:
            for r in range(tm // GRID_W):
                tok = slice(r * GRID_W, (r + 1) * GRID_W)
                grid_row = i * (tm // GRID_W) + r
                cos_ref[tok, :] = rrow_ref[0, pl.ds(grid_row, 1), :] + rcol_ref[0]
                sin_ref[tok, :] = rrow_ref[1, pl.ds(grid_row, 1), :] + rcol_ref[1]

    pair = 2 * LANES
    row = lax.broadcasted_iota(jnp.int32, (pair, pair), 0) // HEAD_DIM
    col = lax.broadcasted_iota(jnp.int32, (pair, pair), 1) // HEAD_DIM
    same_head = (row == col).astype(bf16)
    lane = lax.broadcasted_iota(jnp.int32, (1, LANES), 1)
    quarter = HEAD_DIM // 4
    upper = (lane & quarter) != 0

    def product(w_ref):
        return jnp.dot(h_ref[...], w_ref[...], preferred_element_type=f32)

    def headwise(gain_ref, rope):
        gain = gain_ref[...]
        for part, w_ref in enumerate((wa_ref, wb_ref)):
            res = product(w_ref)
            for s in range(tw // pair):
                y = res[:, s * pair:(s + 1) * pair]
                ssq = jnp.dot((y * y).astype(bf16), same_head, preferred_element_type=f32)
                y = y * lax.rsqrt(ssq * (1.0 / HEAD_DIM) + EPS) * gain
                for u in range(2):
                    yu = y[:, u * LANES:(u + 1) * LANES]
                    if rope:
                        partner = jnp.where(upper, pltpu.roll(yu, quarter, 1), pltpu.roll(yu, LANES - quarter, 1))
                        yu = yu * cos_ref[...] + partner * sin_ref[...]
                    lo = part * tw + s * pair + u * LANES
                    o_ref[:, lo:lo + LANES] = yu.astype(o_ref.dtype)

    def plain():
        o_ref[:, :tw] = product(wa_ref).astype(o_ref.dtype)
        o_ref[:, tw:] = product(wb_ref).astype(o_ref.dtype)

    def multiply(lo):
        o_ref[:, lo:lo + tw] = (product(wa_ref) * product(wb_ref)).astype(o_ref.dtype)

    for idx, kind in enumerate(kinds):
        @pl.when(n == idx)
        def _(kind=kind):
            if kind == "q":
                headwise(qg_ref, True)
            elif kind == "k":
                headwise(kg_ref, True)
            elif kind == "k_nopos":
                headwise(kg_ref, False)
            elif kind == "plain":
                plain()
            elif kind == "mul_lo":
                multiply(0)
            elif kind == "mul_hi":
                multiply(tw)
            else:
                raise ValueError(kind)


def _proj_call(x, mods, g, w, qg, kg, rope, kinds, w_blocks, out_cols, tm, tn=1024):
    m, d = x.shape
    tw = tn // 2
    use_rope = rope is not None
    n_out = max(out_cols) + 1
    blk_a = tuple(b[0] for b in w_blocks)
    blk_b = tuple(b[1] for b in w_blocks)
    out_cols = tuple(out_cols)
    pair_gain = lambda v: jnp.tile(v, 2 * HEADS_PER_SLAB)[None, :]
    assert tm % GRID_W == 0

    in_specs = [
        pl.BlockSpec((tm, d), lambda i, n: (i, 0)),
        pl.BlockSpec((2, d), lambda i, n: (0, 0)),
        pl.BlockSpec((1, d), lambda i, n: (0, 0)),
        pl.BlockSpec((d, tw), lambda i, n: (0, _pick(n, blk_a))),
        pl.BlockSpec((d, tw), lambda i, n: (0, _pick(n, blk_b))),
        pl.BlockSpec((1, 2 * LANES), lambda i, n: (0, 0)),
        pl.BlockSpec((1, 2 * LANES), lambda i, n: (0, 0)),
    ]
    args = [x, mods, g.reshape(1, d), w, w, pair_gain(qg), pair_gain(kg)]
    scratch = [pltpu.VMEM((tm, d), bf16), pltpu.VMEM((tm, LANES), f32)]
    if use_rope:
        in_specs += [pl.BlockSpec(t.shape, lambda i, n: (0, 0, 0)) for t in rope]
        args += list(rope)
        scratch += [pltpu.VMEM((tm, LANES), f32)] * 2
    return pl.pallas_call(
        functools.partial(_proj_kernel, tuple(kinds)),
        grid=(m // tm, len(kinds)),
        in_specs=in_specs,
        out_specs=pl.BlockSpec((tm, tn), lambda i, n: (i, _pick(n, out_cols))),
        out_shape=jax.ShapeDtypeStruct((m, n_out * tn), bf16),
        scratch_shapes=scratch,
        compiler_params=_params("arbitrary", "arbitrary"),
        name="proj",
    )(*args)


def _row_window_tables(rows):
    tables = []
    for r0 in (0, SUB_ROWS, rows - SUB_ROWS):
        ks = min(max(r0 - WIN_H // 2, 0), rows - K_ROWS)
        tab = [[NO_TILE] * K_ROWS for _ in range(SUB_ROWS)]
        for a in range(SUB_ROWS):
            r = r0 + a
            rs = min(max(r - WIN_H // 2, 0), rows - WIN_H)
            assert ks <= rs and rs + WIN_H <= ks + K_ROWS
            for b in range(K_ROWS):
                kr = ks + b
                if rs <= kr < rs + WIN_H:
                    tab[a][b] = kr - r + (WIN_H - 1)
        tables.append(tab)
    return tables


def _attn_kernel(rows, n_side, bounded_ref, q_ref, k_ref, v_ref, kc_ref, vc_ref, rpb_ref, *rest):
    side_in, o_ref, side_out = rest[:n_side], rest[n_side], rest[n_side + 1:2 * n_side + 1]
    tile_ref, bias_ref = rest[2 * n_side + 1:]
    i = pl.program_id(1)
    nblk = pl.num_programs(1)
    tq = SUB_ROWS * GRID_W
    tk = K_ROWS * GRID_W
    lc = kc_ref.shape[0]
    key_tile = 2 * LANES
    lane = lax.broadcasted_iota(jnp.int32, (1, LANES), 1)
    first_head = lane < HEAD_DIM
    ones_tile = jnp.ones((key_tile, LANES), bf16)
    nt = (((1,), (1,)), ((), ()))

    def build_tiles():
        qc = lax.broadcasted_iota(jnp.int32, (GRID_W, LANES), 0)
        kc = lax.broadcasted_iota(jnp.int32, (GRID_W, LANES), 1) % GRID_W
        cs = jnp.clip(qc - WIN_W // 2, 0, GRID_W - WIN_W)
        in_window = (kc >= cs) & (kc < cs + WIN_W)
        for hh in range(HEADS_PER_SLAB):
            for e in range(NO_TILE):
                base = jnp.broadcast_to(rpb_ref[hh, e:e + 1, :], (GRID_W, LANES))
                shifted = pltpu.roll(base, LANES - (WIN_W - 1), 1, stride=1, stride_axis=0)
                tile_ref[hh, e] = jnp.where(in_window, shifted, NEG)
            tile_ref[hh, NO_TILE] = jnp.full((GRID_W, LANES), NEG, f32)

    def assemble(variant, tab):
        for hh in range(HEADS_PER_SLAB):
            for a in range(SUB_ROWS):
                for bp in range(K_ROWS // 2):
                    e0, e1 = tab[a][2 * bp], tab[a][2 * bp + 1]
                    if e0 == e1:
                        tile = tile_ref[hh, e0]
                    else:
                        tile = jnp.where(first_head, tile_ref[hh, e0], tile_ref[hh, e1])
                    lo = hh * tq + a * GRID_W
                    bias_ref[variant, lo:lo + GRID_W, bp * LANES:(bp + 1) * LANES] = tile

    @pl.when(i == 0)
    def _():
        build_tiles()
        for variant, tab in enumerate(_row_window_tables(rows)):
            assemble(variant, tab)

    def sub_block(sb):
        r0 = i * Q_ROWS + sb * SUB_ROWS
        ks = jnp.clip(r0 - WIN_H // 2, 0, rows - K_ROWS) * GRID_W
        ks = pl.multiple_of(ks, SUB_ROWS * GRID_W)
        if sb == 0:
            variant = jnp.where(i == 0, 0, 1)
        elif sb == Q_ROWS // SUB_ROWS - 1:
            variant = jnp.where(i == nblk - 1, 2, 1)
        else:
            variant = 1
        q = q_ref[sb * tq:(sb + 1) * tq, :] * jnp.asarray(HEAD_DIM ** -0.5, bf16)
        zero = jnp.zeros_like(q)
        q_stack = jnp.concatenate([jnp.where(first_head, q, zero), jnp.where(first_head, zero, q)], axis=0)
        return ks, variant, q_stack

    def write_out(sb, prod):
        o = prod[:, :LANES] * (1.0 / prod[:, LANES:])
        o_ref[sb * tq:(sb + 1) * tq, :] = jnp.where(first_head, o[:tq], o[tq:]).astype(o_ref.dtype)

    def key_tiles(ks):
        for lo in range(0, tk + lc, key_tile):
            if lo < tk:
                keys = pl.ds(ks + lo, key_tile)
                yield lo, k_ref[keys, :], v_ref[keys, :]
            else:
                yield lo, kc_ref[lo - tk:lo - tk + key_tile, :], vc_ref[lo - tk:lo - tk + key_tile, :]

    def tile_logits(q_stack, variant, lo, k_tile):
        t = lax.dot_general(q_stack, k_tile, nt, preferred_element_type=f32)
        return t + bias_ref[variant, :, lo:lo + key_tile] if lo < tk else t

    def attend(q_stack, variant, ks, row_max):
        acc = None
        for lo, k_tile, v_tile in key_tiles(ks):
            t = tile_logits(q_stack, variant, lo, k_tile)
            if row_max is not None:
                t = t - row_max
            d = jnp.dot(jnp.exp(t).astype(bf16), jnp.concatenate([v_tile, ones_tile], axis=1),
                        preferred_element_type=f32)
            acc = d if acc is None else acc + d
        return acc

    @pl.when(bounded_ref[0] != 0)
    def _():
        _side_cast(side_in, side_out)
        for sb in range(Q_ROWS // SUB_ROWS):
            ks, variant, q_stack = sub_block(sb)
            write_out(sb, attend(q_stack, variant, ks, None))

    @pl.when(bounded_ref[0] == 0)
    def _():
        _side_cast(side_in, side_out)
        for sb in range(Q_ROWS // SUB_ROWS):
            ks, variant, q_stack = sub_block(sb)
            tile_max = None
            for lo, k_tile, _ in key_tiles(ks):
                t = tile_logits(q_stack, variant, lo, k_tile)
                tile_max = t if tile_max is None else jnp.maximum(tile_max, t)
            row_max = jnp.max(tile_max, axis=-1, keepdims=True)
            write_out(sb, attend(q_stack, variant, ks, row_max))


def _attn_call(bounded, qkv, kvc, rpb_lanes, n_heads, rows, side=()):
    s = qkv.shape[0]
    d_attn = n_heads * HEAD_DIM
    slabs = d_attn // LANES
    tq = SUB_ROWS * GRID_W
    tk = K_ROWS * GRID_W
    nblk = rows // Q_ROWS
    assert rows % Q_ROWS == 0 and Q_ROWS % SUB_ROWS == 0 and rows >= K_ROWS + SUB_ROWS
    assert SUB_ROWS % (WIN_H // 2) == 0
    lc = kvc.shape[0]
    assert tk % (2 * LANES) == 0 and lc % (2 * LANES) == 0
    side_in, side_args, side_out, side_shapes = _side_cast_specs(side)
    grid_spec = pltpu.PrefetchScalarGridSpec(
        num_scalar_prefetch=1,
        grid=(slabs, nblk),
        in_specs=[
            pl.BlockSpec((Q_ROWS * GRID_W, LANES), lambda h, i, b: (i, h)),
            pl.BlockSpec((s, LANES), lambda h, i, b: (0, slabs + h)),
            pl.BlockSpec((s, LANES), lambda h, i, b: (0, 2 * slabs + h)),
            pl.BlockSpec((lc, LANES), lambda h, i, b: (0, h)),
            pl.BlockSpec((lc, LANES), lambda h, i, b: (0, slabs + h)),
            pl.BlockSpec((HEADS_PER_SLAB, N_TILES, LANES), lambda h, i, b: (h, 0, 0)),
        ] + side_in,
        out_specs=[pl.BlockSpec((Q_ROWS * GRID_W, LANES), lambda h, i, b: (i, h))] + side_out,
        scratch_shapes=[
            pltpu.VMEM((HEADS_PER_SLAB, N_TILES, GRID_W, LANES), f32),
            pltpu.VMEM((3, HEADS_PER_SLAB * tq, tk), f32),
        ],
    )
    out = pl.pallas_call(
        functools.partial(_attn_kernel, rows, len(side)),
        grid_spec=grid_spec,
        out_shape=[jax.ShapeDtypeStruct((s, d_attn), bf16)] + side_shapes,
        compiler_params=_params("arbitrary", "arbitrary"),
        name="attn",
    )(bounded, qkv, qkv, qkv, kvc, kvc, rpb_lanes, *side_args)
    return out if side else out[0]


def _rms(x, g):
    ms = jnp.mean(x * x, axis=-1, keepdims=True)
    return x * lax.rsqrt(ms + EPS) * g


def _out_kernel(attn_ref, bg_ref, z_ref, zp_ref, zn_ref, x_ref, cw_ref, cb_ref, ga_ref, gc_ref, gate_ref,
                w_ref, o_ref):
    i = pl.program_id(0)
    nt = pl.num_programs(0)
    tm, dc = z_ref.shape
    da = attn_ref.shape[1]
    z = z_ref[...].astype(f32)
    halo = zp_ref.shape[0]
    before = jnp.where(i > 0, zp_ref[halo - 1:halo, :].astype(f32), 0.0)
    after = jnp.where(i < nt - 1, zn_ref[0:1, :].astype(f32), 0.0)
    row = lax.broadcasted_iota(jnp.int32, (tm, 1), 0)
    z_m = jnp.where(row == 0, before, pltpu.roll(z, 1, 0))
    z_p = jnp.where(row == tm - 1, after, pltpu.roll(z, tm - 1, 0))
    y = cb_ref[...] + z_m * cw_ref[0:1, :]
    y = y + z * cw_ref[1:2, :]
    y = y + z_p * cw_ref[2:3, :]
    conv = bg_ref[...].astype(f32) * y
    cn = _rms(conv, gc_ref[...]).astype(bf16)
    an = _rms(attn_ref[...].astype(f32), ga_ref[...]).astype(bf16)
    proj = (jnp.dot(an, w_ref[0:da, :], preferred_element_type=f32)
            + jnp.dot(cn, w_ref[da:da + dc, :], preferred_element_type=f32))
    o_ref[...] = x_ref[...] + gate_ref[...] * proj


def _out_call(attn, mixed, x, conv_w, conv_b, ga, gc, gate, w_out, tm, bg_col, z_col, halo=16):
    s, d = x.shape
    da = attn.shape[1]
    dc = conv_w.shape[1]
    per = tm // halo
    nh = s // halo
    return pl.pallas_call(
        _out_kernel,
        grid=(s // tm,),
        in_specs=[
            pl.BlockSpec((tm, da), lambda i: (i, 0)),
            pl.BlockSpec((tm, dc), lambda i: (i, bg_col)),
            pl.BlockSpec((tm, dc), lambda i: (i, z_col)),
            pl.BlockSpec((halo, dc), lambda i: (jnp.maximum(i * per - 1, 0), z_col)),
            pl.BlockSpec((halo, dc), lambda i: (jnp.minimum((i + 1) * per, nh - 1), z_col)),
            pl.BlockSpec((tm, d), lambda i: (i, 0)),
            pl.BlockSpec((3, dc), lambda i: (0, 0)),
            pl.BlockSpec((1, dc), lambda i: (0, 0)),
            pl.BlockSpec((1, da), lambda i: (0, 0)),
            pl.BlockSpec((1, dc), lambda i: (0, 0)),
            pl.BlockSpec((1, d), lambda i: (0, 0)),
            pl.BlockSpec((da + dc, d), lambda i: (0, 0)),
        ],
        out_specs=pl.BlockSpec((tm, d), lambda i: (i, 0)),
        out_shape=jax.ShapeDtypeStruct((s, d), f32),
        compiler_params=_params("arbitrary"),
        name="out",
    )(attn, mixed, mixed, mixed, mixed, x, conv_w, conv_b.reshape(1, dc), ga.reshape(1, da),
      gc.reshape(1, dc), gate.reshape(1, d), w_out)


def _rope_tables(rows):
    nf = HEAD_DIM // 4
    inv = ROPE_BASE ** (-jnp.arange(nf, dtype=f32) / nf)
    zero = jnp.zeros((1, nf), f32)

    def lanes(pos, first_half):
        ang = jnp.arange(pos, dtype=jnp.int32).astype(f32)[:, None] * inv[None, :]
        cos, sin = jnp.cos(ang), jnp.sin(ang)
        z = jnp.broadcast_to(zero, cos.shape)
        cos_head = [cos, cos, z, z] if first_half else [z, z, cos, cos]
        sin_head = [-sin, sin, z, z] if first_half else [z, z, -sin, sin]
        table = lambda parts: jnp.tile(jnp.concatenate(parts, axis=-1), (1, HEADS_PER_SLAB))
        return jnp.stack([table(cos_head), table(sin_head)])

    return lanes(rows, True), lanes(GRID_W, False)


def _logits_bounded(q_gain, k_gain, rpb):
    bound = (HEAD_DIM ** 0.5) * 1.02 * jnp.max(jnp.abs(q_gain)) * jnp.max(jnp.abs(k_gain)) + jnp.max(jnp.abs(rpb))
    return (bound <= LOGIT_BOUND).astype(jnp.int32).reshape(1)


def _rpb_lanes(rpb):
    h, nr, ncol = rpb.shape
    half = jnp.pad(rpb, ((0, 0), (0, N_TILES - nr), (0, GRID_W - ncol)))
    return jnp.concatenate([half, half], axis=-1)


def kernel(x, c, ctx, c_ctx, w_ada, b_ada, ff1_norm, ff1_w_in, ff1_w_out, mix_norm, w_in, q_norm, k_norm,
           rpb, conv_w, conv_b, out_norm_attn, out_norm_conv, w_out, ff2_norm, ff2_w_in, ff2_w_out):
    batch, s, d = x.shape
    depth = w_ada.shape[0]
    assert batch == 1 and depth == 1
    n_heads = rpb.shape[1]
    d_attn = n_heads * HEAD_DIM
    d_conv = conv_w.shape[-1]
    assert d_attn == 1024 and d_conv == 1024
    rows = s // GRID_W

    xs = x[0]
    cs = ctx[0]
    l = 0

    mods = _ada_call(jnp.stack([c[0], c_ctx], axis=1), w_ada[l], b_ada[l])
    mx = mods[0].reshape(N_MOD, d)
    mc = mods[1].reshape(N_MOD, d)

    w1_in = ff1_w_in[l].astype(bf16)
    w1_out = ff1_w_out[l].astype(bf16)

    rope = _rope_tables(rows)

    tm_ffn = 1024
    strip = d // (s // tm_ffn)
    mix_blocks = w_in.shape[-1] // 1024
    out_blocks = w_out.shape[-1] // 1024
    xs, w_mix, w_o = _ffn_call(
        xs, mx[0:3], ff1_norm[l], w1_in, w1_out, tm=tm_ffn, tf=512,
        side=[(w_in[l], (strip, 1024), lambda i, j: (i, jnp.minimum(j, mix_blocks - 1))),
              (w_out[l], (strip, 1024), lambda i, j: (i, jnp.minimum(j, out_blocks - 1)))])
    cs = _ffn_call(cs, mc[0:3], ff1_norm[l], w1_in, w1_out, tm=cs.shape[0], tf=1408)

    mixed = _proj_call(xs, mx[3:5], mix_norm[l], w_mix, q_norm[l], k_norm[l], rope,
                       kinds=("q", "k", "plain", "plain", "mul_lo", "mul_hi"),
                       w_blocks=((0, 1), (2, 3), (4, 5), (6, 7), (8, 10), (9, 11)),
                       out_cols=(0, 1, 2, 3, 4, 4), tm=1024)
    kvc = _proj_call(cs, mc[3:5], mix_norm[l], w_mix, q_norm[l], k_norm[l], None,
                     kinds=("k_nopos", "plain"), w_blocks=((2, 3), (4, 5)), out_cols=(0, 1),
                     tm=cs.shape[0])

    slabs, nblk = d_attn // LANES, rows // Q_ROWS
    w2i, w2o = ff2_w_in[l], ff2_w_out[l]
    attn, w2_in, w2_out = _attn_call(
        _logits_bounded(q_norm[l], k_norm[l], rpb[l]), mixed, kvc, _rpb_lanes(rpb[l]), n_heads, rows,
        side=[(w2i, (w2i.shape[0] // slabs, w2i.shape[1] // nblk), lambda h, i, b: (h, i)),
              (w2o, (w2o.shape[0] // slabs, w2o.shape[1] // nblk), lambda h, i, b: (h, i))])
    xs = _out_call(attn, mixed, xs, conv_w[l], conv_b[l], out_norm_attn[l], out_norm_conv[l], mx[5], w_o,
                   tm=512, bg_col=3, z_col=4)

    xs = _ffn_call(xs, mx[6:9], ff2_norm[l], w2_in, w2_out, tm=1024, tf=512)
    return xs[None]
```
